```python
import math
import jax
import jax.numpy as jnp
from jax import lax
import numpy as np

D_MODEL = 1024
BATCH = 4
SEQ = 4096
DEPTH = 2
DEC_BATCH = 32
DEC_SEQ = 4
PAST_LEN = 16384
PAGE_SIZE = 128

BRANCH_WIDTH = D_MODEL // 2
HEAD_DIM = 64
N_HEADS = BRANCH_WIDTH // HEAD_DIM
ATT_WIDTH = N_HEADS * HEAD_DIM
MOBA_BLOCK = 256
MOBA_TOP_K = 3
ROPE_THETA = 10000.0
Q_BLOCK = 32
CM_CHUNK = 128
CM_GROUPS = 8
CM_WIDTH = BRANCH_WIDTH
CM_GROUP_WIDTH = CM_WIDTH // CM_GROUPS
POOL_WINDOWS = (2, 4, 8, 16)
POOL_WIDTH = BRANCH_WIDTH
POOL_GROUP_WIDTH = POOL_WIDTH // len(POOL_WINDOWS)
POOL_KEEP = max(POOL_WINDOWS) - 1
N_BRANCH = 3
D_FF = -(-8 * D_MODEL // (3 * 256)) * 256
IN_WIDTH = 2 * CM_WIDTH + POOL_WIDTH + 3 * ATT_WIDTH + N_BRANCH * D_MODEL
NORM_EPS = 1e-6
NEG = -1e30

kernel_name = 'hybrid_gated_chunkmlp_pool_moba_step'


def rmsnorm(x, g):
    x32 = x.astype(jnp.float32)
    y = x32 * lax.rsqrt(jnp.mean(x32 * x32, axis=-1, keepdims=True) + NORM_EPS)
    return (y * g.astype(jnp.float32)).astype(x.dtype)


def layernorm(x, g):
    x32 = x.astype(jnp.float32)
    mu = jnp.mean(x32, axis=-1, keepdims=True)
    xc = x32 - mu
    y = xc * lax.rsqrt(jnp.mean(xc * xc, axis=-1, keepdims=True) + NORM_EPS)
    return (y * g.astype(jnp.float32)).astype(x.dtype)


def rope(x, pos):
    half = HEAD_DIM // 2
    inv = ROPE_THETA ** (-jnp.arange(half, dtype=jnp.float32) / half)
    ang = pos.astype(jnp.float32)[:, None] * inv[None, :]
    cos = jnp.cos(ang)[None, :, None, :]
    sin = jnp.sin(ang)[None, :, None, :]
    x32 = x.astype(jnp.float32)
    x1, x2 = x32[..., :half], x32[..., half:]
    return jnp.concatenate([x1 * cos - x2 * sin, x2 * cos + x1 * sin], axis=-1).astype(x.dtype)


def mixer_projections(x, pos, w_in_l, g_pre_l):
    n, t, _ = x.shape
    z = rmsnorm(x, g_pre_l) @ w_in_l
    offs = np.cumsum([CM_WIDTH, CM_WIDTH, POOL_WIDTH, ATT_WIDTH, ATT_WIDTH, ATT_WIDTH])
    u, v, p, q, k, va, g = jnp.split(z, offs, axis=-1)
    q = rope(q.reshape(n, t, N_HEADS, HEAD_DIM), pos)
    k = rope(k.reshape(n, t, N_HEADS, HEAD_DIM), pos)
    va = va.reshape(n, t, N_HEADS, HEAD_DIM)
    gates = jax.nn.sigmoid(g.reshape(n, t, N_BRANCH, D_MODEL))
    return u, v, p, q, k, va, gates


def chunk_mlp(u_raw, v_raw, g_v_l, w_s_l, b_s_l, clen):
    n, t, _ = u_raw.shape
    u = jax.nn.gelu(u_raw)
    vn = layernorm(jax.nn.gelu(v_raw), g_v_l)
    tri = jnp.asarray(np.tril(np.ones((clen, clen), np.float32)), dtype=w_s_l.dtype)
    w = w_s_l[:, :clen, :clen] * tri
    vc = vn.reshape(n, t // clen, clen, CM_GROUPS, CM_GROUP_WIDTH)
    s = jnp.einsum('gts,ncsge->nctge', w, vc) + b_s_l[:, :clen].T[:, :, None]
    return u * s.reshape(n, t, CM_WIDTH), vn


def pool_mixer(p, prev, w_pool_l, scale_l):
    n, s_len, _ = p.shape
    ext = p if prev is None else jnp.concatenate([prev.astype(p.dtype), p], axis=1)
    n_prev = ext.shape[1] - s_len
    cs = jnp.pad(jnp.cumsum(ext.astype(jnp.float32), axis=1), ((0, 0), (1, 0), (0, 0)))
    end = n_prev + np.arange(s_len) + 1
    outs = []
    for gi, win in enumerate(POOL_WINDOWS):
        start = np.maximum(end - win, 0)
        cnt = (end - start).astype(np.float32)[None, :, None]
        sl = slice(gi * POOL_GROUP_WIDTH, (gi + 1) * POOL_GROUP_WIDTH)
        mean = (cs[:, end, sl] - cs[:, start, sl]) / cnt
        outs.append(mean - p[:, :, sl].astype(jnp.float32))
    d = jnp.stack(outs, axis=2).astype(p.dtype)
    y = jnp.einsum('nsgc,gcd->nsgd', d, w_pool_l).reshape(n, s_len, POOL_WIDTH) * scale_l
    return y, ext[:, -POOL_KEEP:]


def moba_attend(q, t, own_blk, own_k, own_v, kmeans, fetch_sel):
    scale = HEAD_DIM ** -0.5
    own_pos = own_blk * MOBA_BLOCK + jnp.arange(MOBA_BLOCK)
    own_ok = (own_pos[None, :] <= t[:, None])[None, :, None, :]
    own_logit = jnp.where(own_ok, jnp.einsum('nshd,nhkd->nshk', q, own_k, preferred_element_type=jnp.float32) * scale, NEG)
    if kmeans is None:
        p = jax.nn.softmax(own_logit, axis=-1).astype(own_v.dtype)
        return jnp.einsum('nshk,nhkd->nshd', p, own_v)
    n_cand = kmeans.shape[1]
    n_sel = min(MOBA_TOP_K, n_cand)
    gate = jnp.einsum('nshd,njhd->nshj', q.astype(jnp.float32), kmeans)
    gate = jnp.where(jnp.arange(n_cand) < own_blk, gate, NEG)
    _, idx = lax.top_k(gate, n_sel)
    sel_ok = (idx < own_blk)[..., None]
    ks, vs = fetch_sel(idx)
    sel_logit = jnp.where(sel_ok, jnp.einsum('nshd,nshjkd->nshjk', q, ks, preferred_element_type=jnp.float32) * scale, NEG)
    n, s_len, h = idx.shape[:3]
    logits = jnp.concatenate([sel_logit.reshape(n, s_len, h, n_sel * MOBA_BLOCK), own_logit], axis=-1)
    p = jax.nn.softmax(logits, axis=-1).astype(vs.dtype)
    p_sel = p[..., :n_sel * MOBA_BLOCK].reshape(n, s_len, h, n_sel, MOBA_BLOCK)
    p_own = p[..., n_sel * MOBA_BLOCK:]
    return jnp.einsum('nshjk,nshjkd->nshd', p_sel, vs) + jnp.einsum('nshk,nhkd->nshd', p_own, own_v)


def moba_prompt(q, k, v):
    n, t_len = q.shape[:2]
    nb = -(-t_len // MOBA_BLOCK)
    pad = nb * MOBA_BLOCK - t_len

    def to_blocks(a):
        a = jnp.pad(a, ((0, 0), (0, pad), (0, 0), (0, 0)))
        return a.reshape(n, nb, MOBA_BLOCK, N_HEADS, HEAD_DIM).transpose(0, 3, 1, 2, 4)

    kb, vb = to_blocks(k), to_blocks(v)
    n_full = (t_len - 1) // MOBA_BLOCK
    kmeans = kb[:, :, :n_full].astype(jnp.float32).mean(axis=3).transpose(0, 2, 1, 3) if n_full > 0 else None
    bi = jnp.arange(n)[:, None, None, None]
    hi = jnp.arange(N_HEADS)[None, None, :, None]

    def fetch_sel(idx):
        return kb[bi, hi, idx], vb[bi, hi, idx]

    def step(args):
        qc, c = args
        start = c * Q_BLOCK
        tq = start + jnp.arange(Q_BLOCK)
        b = start // MOBA_BLOCK
        own_k = lax.dynamic_index_in_dim(kb, b, axis=2, keepdims=False)
        own_v = lax.dynamic_index_in_dim(vb, b, axis=2, keepdims=False)
        return moba_attend(qc, tq, b, own_k, own_v, kmeans, fetch_sel)

    qs = q.reshape(n, t_len // Q_BLOCK, Q_BLOCK, N_HEADS, HEAD_DIM).swapaxes(0, 1)
    out = lax.map(step, (qs, jnp.arange(t_len // Q_BLOCK)))
    return out.swapaxes(0, 1).reshape(n, t_len, ATT_WIDTH)


def moba_sample(q, k_new, v_new, cache_k, cache_v, page_table, l):
    n, s_len = q.shape[:2]
    ppb = MOBA_BLOCK // PAGE_SIZE
    own_blk = PAST_LEN // MOBA_BLOCK
    tail = PAST_LEN - own_blk * MOBA_BLOCK
    tq = PAST_LEN + jnp.arange(s_len)

    def own_rows(cache, new):
        parts = []
        if tail > 0:
            pg = cache[l, page_table[:, own_blk * ppb:]]
            parts.append(pg.transpose(0, 2, 1, 3, 4).reshape(n, N_HEADS, tail, HEAD_DIM))
        parts.append(new.transpose(0, 2, 1, 3).astype(cache.dtype))
        rows = jnp.concatenate(parts, axis=2)
        return jnp.pad(rows, ((0, 0), (0, 0), (0, MOBA_BLOCK - tail - s_len), (0, 0)))

    own_k, own_v = own_rows(cache_k, k_new), own_rows(cache_v, v_new)
    kmeans = None
    if own_blk > 0:
        pk = cache_k[l, page_table[:, :own_blk * ppb]]
        kmeans = pk.astype(jnp.float32).reshape(n, own_blk, ppb, N_HEADS, PAGE_SIZE, HEAD_DIM).mean(axis=(2, 4))
    bi = jnp.arange(n)[:, None, None, None, None]
    hi = jnp.arange(N_HEADS)[None, None, :, None, None]

    def fetch_sel(idx):
        pages = page_table[bi, idx[..., None] * ppb + jnp.arange(ppb)]
        shp = idx.shape + (MOBA_BLOCK, HEAD_DIM)
        return cache_k[l, pages, hi].reshape(shp), cache_v[l, pages, hi].reshape(shp)

    return moba_attend(q, tq, own_blk, own_k, own_v, kmeans, fetch_sel).reshape(n, s_len, ATT_WIDTH)


def finish_layer(x, gates, o_a, o_b, o_c, w_branch_l, w_out_l, g_post_mix_l, g_pre_ffn_l, g_post_ffn_l, w_ff_in_l, w_ff_out_l):
    br = jnp.einsum('ntic,icd->ntid', jnp.stack([o_a, o_b, o_c], axis=2), w_branch_l)
    mix = jnp.sum(gates * br, axis=2) @ w_out_l
    x = x + rmsnorm(mix, g_post_mix_l)
    gt, up = jnp.split(rmsnorm(x, g_pre_ffn_l) @ w_ff_in_l, 2, axis=-1)
    return x + rmsnorm((jax.nn.silu(gt) * up) @ w_ff_out_l, g_post_ffn_l)


def to_pages(a):
    n, t_len = a.shape[:2]
    return a.reshape(n, t_len // PAGE_SIZE, PAGE_SIZE, N_HEADS, HEAD_DIM).transpose(0, 1, 3, 2, 4)


def setup_inputs(seed: int = 0) -> dict:
    key = jax.random.key(seed)
    ks = jax.random.split(key, 24)
    n_pages = PAST_LEN // PAGE_SIZE
    n_used = DEC_BATCH * n_pages
    n_pool = n_used + max(1, n_used // 4)
    nrm = jax.random.normal
    f32 = jnp.float32
    page_table = jax.random.permutation(ks[0], n_pool)[:n_used].reshape(DEC_BATCH, n_pages).astype(jnp.int32)
    return {
        'x_prompt': nrm(ks[1], (BATCH, SEQ, D_MODEL), f32),
        'x_sample': nrm(ks[2], (DEC_BATCH, DEC_SEQ, D_MODEL), f32),
        'cache_k': nrm(ks[3], (DEPTH, n_pool, N_HEADS, PAGE_SIZE, HEAD_DIM), f32),
        'cache_v': nrm(ks[4], (DEPTH, n_pool, N_HEADS, PAGE_SIZE, HEAD_DIM), f32),
        'state_pool': nrm(ks[5], (DEPTH, DEC_BATCH, POOL_KEEP, POOL_WIDTH), f32),
        'page_table': page_table,
        'w_in': nrm(ks[6], (DEPTH, D_MODEL, IN_WIDTH), f32) * D_MODEL ** -0.5,
        'g_v': 1.0 + 0.02 * nrm(ks[7], (DEPTH, CM_WIDTH), f32),
        'w_s': nrm(ks[8], (DEPTH, CM_GROUPS, CM_CHUNK, CM_CHUNK), f32) * CM_CHUNK ** -0.5,
        'b_s': 1.0 + 0.02 * nrm(ks[9], (DEPTH, CM_GROUPS, CM_CHUNK), f32),
        'w_pool': nrm(ks[10], (DEPTH, len(POOL_WINDOWS), POOL_GROUP_WIDTH, POOL_GROUP_WIDTH), f32) * POOL_GROUP_WIDTH ** -0.5,
        'pool_scale': 1.0 + 0.02 * nrm(ks[11], (DEPTH, POOL_WIDTH), f32),
        'w_branch': nrm(ks[12], (DEPTH, N_BRANCH, BRANCH_WIDTH, D_MODEL), f32) * BRANCH_WIDTH ** -0.5,
        'w_out': nrm(ks[13], (DEPTH, D_MODEL, D_MODEL), f32) * D_MODEL ** -0.5,
        'g_pre_mix': 1.0 + 0.02 * nrm(ks[14], (DEPTH, D_MODEL), f32),
        'g_post_mix': 1.0 + 0.02 * nrm(ks[15], (DEPTH, D_MODEL), f32),
        'g_pre_ffn': 1.0 + 0.02 * nrm(ks[16], (DEPTH, D_MODEL), f32),
        'g_post_ffn': 1.0 + 0.02 * nrm(ks[17], (DEPTH, D_MODEL), f32),
        'w_ff_in': nrm(ks[18], (DEPTH, D_MODEL, 2 * D_FF), f32) * D_MODEL ** -0.5,
        'w_ff_out': nrm(ks[19], (DEPTH, D_FF, D_MODEL), f32) * D_FF ** -0.5,
    }


def reference(x_prompt, x_sample, cache_k, cache_v, state_pool, page_table, w_in, g_v, w_s, b_s, w_pool, pool_scale, w_branch, w_out, g_pre_mix, g_post_mix, g_pre_ffn, g_post_ffn, w_ff_in, w_ff_out):
    pos_p = jnp.arange(x_prompt.shape[1])
    pos_s = PAST_LEN + jnp.arange(x_sample.shape[1])
    xp, xs = x_prompt, x_sample
    kp_l, vp_l, poolp_l, ks_l, vs_l, pools_l, cvs_l = [], [], [], [], [], [], []
    for l in range(DEPTH):
        u, v, p, q, k, va, gates = mixer_projections(xp, pos_p, w_in[l], g_pre_mix[l])
        o_a, _ = chunk_mlp(u, v, g_v[l], w_s[l], b_s[l], CM_CHUNK)
        o_b, pool_new = pool_mixer(p, None, w_pool[l], pool_scale[l])
        o_c = moba_prompt(q, k, va)
        xp = finish_layer(xp, gates, o_a, o_b, o_c, w_branch[l], w_out[l], g_post_mix[l], g_pre_ffn[l], g_post_ffn[l], w_ff_in[l], w_ff_out[l])
        kp_l.append(to_pages(k))
        vp_l.append(to_pages(va))
        poolp_l.append(pool_new)
        u, v, p, q, k, va, gates = mixer_projections(xs, pos_s, w_in[l], g_pre_mix[l])
        o_a, cm_state = chunk_mlp(u, v, g_v[l], w_s[l], b_s[l], xs.shape[1])
        o_b, pool_new = pool_mixer(p, state_pool[l], w_pool[l], pool_scale[l])
        o_c = moba_sample(q, k, va, cache_k, cache_v, page_table, l)
        xs = finish_layer(xs, gates, o_a, o_b, o_c, w_branch[l], w_out[l], g_post_mix[l], g_pre_ffn[l], g_post_ffn[l], w_ff_in[l], w_ff_out[l])
        ks_l.append(k.transpose(0, 2, 1, 3))
        vs_l.append(va.transpose(0, 2, 1, 3))
        pools_l.append(pool_new)
        cvs_l.append(cm_state)
    return (xp, xs, jnp.stack(kp_l), jnp.stack(vp_l), jnp.stack(poolp_l), jnp.stack(ks_l), jnp.stack(vs_l), jnp.stack(pools_l), jnp.stack(cvs_l))
```

```python
import functools

import numpy as np
import jax
import jax.numpy as jnp
from jax import lax
from jax.experimental import pallas as pl
from jax.experimental.pallas import tpu as pltpu

D_MODEL = 1024
BATCH = 4
SEQ = 4096
DEPTH = 2
DEC_BATCH = 32
DEC_SEQ = 4
PAST_LEN = 16384
PAGE_SIZE = 128

BRANCH_WIDTH = D_MODEL // 2
HEAD_DIM = 64
N_HEADS = BRANCH_WIDTH // HEAD_DIM
MOBA_BLOCK = 256
MOBA_TOP_K = 3
ROPE_THETA = 10000.0
CM_CHUNK = 128
CM_GROUPS = 8
CM_GROUP_WIDTH = BRANCH_WIDTH // CM_GROUPS
POOL_WINDOWS = (2, 4, 8, 16)
POOL_GROUP_WIDTH = BRANCH_WIDTH // len(POOL_WINDOWS)
POOL_KEEP = max(POOL_WINDOWS) - 1
N_BRANCH = 3
D_FF = -(-8 * D_MODEL // (3 * 256)) * 256
IN_WIDTH = 6 * BRANCH_WIDTH + N_BRANCH * D_MODEL
NORM_EPS = 1e-6
NEG = -1e30

LANES = 128
COL_TILE = BRANCH_WIDTH
N_COL_TILES = IN_WIDTH // COL_TILE
GATE_COL0 = 6 * BRANCH_WIDTH
PAGES_PER_BLOCK = MOBA_BLOCK // PAGE_SIZE
N_PAST_BLOCKS = PAST_LEN // MOBA_BLOCK
N_PROMPT_BLOCKS = SEQ // MOBA_BLOCK
FF_TILE = D_FF // 2
VMEM_LIMIT = 56 * 1024 * 1024

F32 = jnp.float32
BF16 = jnp.bfloat16


def _cparams(n_axes):
    return pltpu.CompilerParams(dimension_semantics=("arbitrary",) * n_axes,
                                vmem_limit_bytes=VMEM_LIMIT)


def _rms(x, g):
    return (x * lax.rsqrt(jnp.mean(x * x, axis=-1, keepdims=True) + NORM_EPS)) * g


def _sigmoid(x):
    return 1.0 / (1.0 + jnp.exp(-x))


def _inproj_kernel(x_ref, g_ref, w_ref, gv_ref, cos_ref, sin_ref, z_ref, xn_ref):
    j = pl.program_id(1)

    @pl.when(j == 0)
    def _():
        xn_ref[...] = _rms(x_ref[...], g_ref[...]).astype(BF16)

    acc = jnp.dot(xn_ref[...], w_ref[...], preferred_element_type=F32)

    @pl.when(j == 0)
    def _():
        z_ref[...] = jax.nn.gelu(acc)

    @pl.when(j == 1)
    def _():
        v = jax.nn.gelu(acc)
        vc = v - jnp.mean(v, axis=-1, keepdims=True)
        y = vc * lax.rsqrt(jnp.mean(vc * vc, axis=-1, keepdims=True) + NORM_EPS)
        z_ref[...] = y * gv_ref[...]

    @pl.when((j == 2) | (j == 5))
    def _():
        z_ref[...] = acc

    @pl.when((j == 3) | (j == 4))
    def _():
        lane = lax.broadcasted_iota(jnp.int32, (1, LANES), 1)
        first_half = (lane % HEAD_DIM) < (HEAD_DIM // 2)
        cos = cos_ref[...]
        sin = sin_ref[...]
        for c in range(COL_TILE // LANES):
            a = acc[:, c * LANES:(c + 1) * LANES]
            partner = jnp.where(first_half,
                                pltpu.roll(a, LANES - HEAD_DIM // 2, axis=1),
                                pltpu.roll(a, HEAD_DIM // 2, axis=1))
            z_ref[:, c * LANES:(c + 1) * LANES] = a * cos + partner * sin

    @pl.when(j >= GATE_COL0 // COL_TILE)
    def _():
        z_ref[...] = _sigmoid(acc)


def _inproj(x2d, g_pre, w_in_bf, g_v, cos_t, sin_t, tm):
    rows = x2d.shape[0]
    t_tiles = cos_t.shape[0] // tm
    return pl.pallas_call(
        _inproj_kernel,
        grid=(rows // tm, N_COL_TILES),
        in_specs=[
            pl.BlockSpec((tm, D_MODEL), lambda i, j: (i, 0)),
            pl.BlockSpec((1, D_MODEL), lambda i, j: (0, 0)),
            pl.BlockSpec((D_MODEL, COL_TILE), lambda i, j: (0, j)),
            pl.BlockSpec((1, COL_TILE), lambda i, j: (0, 0)),
            pl.BlockSpec((tm, LANES), lambda i, j: (i % t_tiles, 0)),
            pl.BlockSpec((tm, LANES), lambda i, j: (i % t_tiles, 0)),
        ],
        out_specs=pl.BlockSpec((tm, COL_TILE), lambda i, j: (i, j)),
        out_shape=jax.ShapeDtypeStruct((rows, IN_WIDTH), F32),
        scratch_shapes=[pltpu.VMEM((tm, D_MODEL), BF16)],
        compiler_params=_cparams(2),
        name="inproj",
    )(x2d, g_pre, w_in_bf, g_v, cos_t, sin_t)


MIX_TILE = 2 * CM_CHUNK
HALO = 16


def _prompt_mix_kernel(u_ref, vn_ref, p_ref, prev_ref, ws_ref, bs_ref, wp_ref, sc_ref,
                       oab_ref, ext_ref):
    i = pl.program_id(0)
    tiles_per_seq = SEQ // MIX_TILE
    first = (i % tiles_per_seq) == 0

    row = lax.broadcasted_iota(jnp.int32, (CM_CHUNK, CM_CHUNK), 0)
    col = lax.broadcasted_iota(jnp.int32, (CM_CHUNK, CM_CHUNK), 1)
    tri = row >= col
    lane = lax.broadcasted_iota(jnp.int32, (1, LANES), 1)
    lo_half = lane < CM_GROUP_WIDTH
    w_tri = [jnp.where(tri, ws_ref[g], 0.0).astype(BF16) for g in range(CM_GROUPS)]
    for c in range(MIX_TILE // CM_CHUNK):
        rs = slice(c * CM_CHUNK, (c + 1) * CM_CHUNK)
        for pr in range(CM_GROUPS // 2):
            cs = slice(pr * LANES, (pr + 1) * LANES)
            vn = vn_ref[rs, cs]
            v_lo = jnp.where(lo_half, vn, 0.0).astype(BF16)
            v_hi = jnp.where(lo_half, 0.0, vn).astype(BF16)
            s = (jnp.dot(w_tri[2 * pr], v_lo, preferred_element_type=F32)
                 + jnp.dot(w_tri[2 * pr + 1], v_hi, preferred_element_type=F32))
            bias = jnp.where(lo_half, bs_ref[:, 2 * pr:2 * pr + 1], bs_ref[:, 2 * pr + 1:2 * pr + 2])
            oab_ref[rs, cs] = u_ref[rs, cs] * (s + bias)

    ext_ref[0:HALO, :] = jnp.where(first, 0.0, prev_ref[...])
    ext_ref[HALO:HALO + MIX_TILE, :] = p_ref[...]
    t_in_seq = (i % tiles_per_seq) * MIX_TILE + lax.broadcasted_iota(jnp.int32, (MIX_TILE, 1), 0)
    for gi, win in enumerate(POOL_WINDOWS):
        cs = slice(gi * POOL_GROUP_WIDTH, (gi + 1) * POOL_GROUP_WIDTH)
        tot = ext_ref[HALO:HALO + MIX_TILE, cs]
        for k in range(1, win):
            tot = tot + ext_ref[HALO - k:HALO - k + MIX_TILE, cs]
        cnt = jnp.minimum(t_in_seq + 1, win).astype(F32)
        d = tot / cnt - p_ref[:, cs]
        y = jnp.dot(d.astype(BF16), wp_ref[gi], preferred_element_type=F32)
        oab_ref[:, BRANCH_WIDTH + gi * POOL_GROUP_WIDTH:BRANCH_WIDTH + (gi + 1) * POOL_GROUP_WIDTH] = y * sc_ref[:, cs]


def _prompt_mix(z, w_s, b_s_t, w_pool_bf, pool_scale):
    rows = z.shape[0]
    halo_blocks = MIX_TILE // HALO
    return pl.pallas_call(
        _prompt_mix_kernel,
        grid=(rows // MIX_TILE,),
        in_specs=[
            pl.BlockSpec((MIX_TILE, BRANCH_WIDTH), lambda i: (i, 0)),
            pl.BlockSpec((MIX_TILE, BRANCH_WIDTH), lambda i: (i, 1)),
            pl.BlockSpec((MIX_TILE, BRANCH_WIDTH), lambda i: (i, 2)),
            pl.BlockSpec((HALO, BRANCH_WIDTH), lambda i: (jnp.maximum(i * halo_blocks - 1, 0), 2)),
            pl.BlockSpec((CM_GROUPS, CM_CHUNK, CM_CHUNK), lambda i: (0, 0, 0)),
            pl.BlockSpec((CM_CHUNK, CM_GROUPS), lambda i: (0, 0)),
            pl.BlockSpec((len(POOL_WINDOWS), POOL_GROUP_WIDTH, POOL_GROUP_WIDTH), lambda i: (0, 0, 0)),
            pl.BlockSpec((1, BRANCH_WIDTH), lambda i: (0, 0)),
        ],
        out_specs=pl.BlockSpec((MIX_TILE, 2 * BRANCH_WIDTH), lambda i: (i, 0)),
        out_shape=jax.ShapeDtypeStruct((rows, 2 * BRANCH_WIDTH), F32),
        scratch_shapes=[pltpu.VMEM((HALO + MIX_TILE, BRANCH_WIDTH), F32)],
        compiler_params=_cparams(1),
        name="prompt_mix",
    )(z, z, z, z, w_s, b_s_t, w_pool_bf, pool_scale)


def _top_k_mask(gate, lane_f, n_valid):
    n_lanes = gate.shape[-1]
    g = jnp.where(lane_f < n_valid, gate, NEG)
    sel = jnp.zeros(gate.shape, F32)
    for _ in range(MOBA_TOP_K):
        mx = jnp.max(g, axis=-1, keepdims=True)
        first = jnp.min(jnp.where(g == mx, lane_f, float(n_lanes)), axis=-1, keepdims=True)
        pick = jnp.where(lane_f == first, jnp.where(mx > NEG, 1.0, 0.0), 0.0)
        sel = jnp.maximum(sel, pick)
        g = jnp.where(pick > 0.0, NEG, g)
    return sel


def _prompt_attn_kernel(q_ref, k_ref, v_ref, o_ref, kbf_ref, vbf_ref, km_ref):
    b = pl.program_id(2)

    @pl.when(b == 0)
    def _():
        kbf_ref[...] = k_ref[...].astype(BF16)
        vbf_ref[...] = v_ref[...].astype(BF16)
        for j in range(N_PROMPT_BLOCKS):
            km_ref[j:j + 1, :] = jnp.mean(k_ref[j * MOBA_BLOCK:(j + 1) * MOBA_BLOCK, :], axis=0, keepdims=True)

    q = q_ref[...] * (HEAD_DIM ** -0.5)
    lane = lax.broadcasted_iota(jnp.int32, (1, LANES), 1)
    blk_f = lax.broadcasted_iota(jnp.int32, (MOBA_BLOCK, N_PROMPT_BLOCKS), 1).astype(F32)
    row = lax.broadcasted_iota(jnp.int32, (MOBA_BLOCK, MOBA_BLOCK), 0)
    col = lax.broadcasted_iota(jnp.int32, (MOBA_BLOCK, MOBA_BLOCK), 1)
    causal = col <= row
    nt = (((1,), (1,)), ((), ()))
    own0 = pl.multiple_of(b * MOBA_BLOCK, MOBA_BLOCK)
    k_own = kbf_ref[pl.ds(own0, MOBA_BLOCK), :]
    v_own = vbf_ref[pl.ds(own0, MOBA_BLOCK), :]
    b_f = b.astype(F32)

    out = jnp.zeros((MOBA_BLOCK, LANES), F32)
    for hh in range(LANES // HEAD_DIM):
        head = (lane // HEAD_DIM) == hh
        qh = jnp.where(head, q, 0.0)
        qh_bf = qh.astype(BF16)
        gate = lax.dot_general(qh, km_ref[...], nt, precision=lax.Precision.HIGHEST,
                               preferred_element_type=F32)
        sel = _top_k_mask(gate, blk_f, b_f)

        s = lax.dot_general(qh_bf, k_own, nt, preferred_element_type=F32)
        s = jnp.where(causal, s, NEG)
        m = jnp.max(s, axis=-1, keepdims=True)
        p = jnp.exp(s - m)
        l = jnp.sum(p, axis=-1, keepdims=True)
        acc = jnp.dot(p.astype(BF16), v_own, preferred_element_type=F32)

        def body(j, carry):
            m, l, acc = carry
            r0 = pl.multiple_of(j * MOBA_BLOCK, MOBA_BLOCK)
            kj = kbf_ref[pl.ds(r0, MOBA_BLOCK), :]
            vj = vbf_ref[pl.ds(r0, MOBA_BLOCK), :]
            sj = lax.dot_general(qh_bf, kj, nt, preferred_element_type=F32)
            chosen = jnp.max(jnp.where(blk_f == j.astype(F32), sel, 0.0), axis=-1, keepdims=True)
            sj = jnp.where(chosen > 0.0, sj, NEG)
            m_new = jnp.maximum(m, jnp.max(sj, axis=-1, keepdims=True))
            alpha = jnp.exp(m - m_new)
            pj = jnp.exp(sj - m_new)
            l = alpha * l + jnp.sum(pj, axis=-1, keepdims=True)
            acc = alpha * acc + jnp.dot(pj.astype(BF16), vj, preferred_element_type=F32)
            return m_new, l, acc

        m, l, acc = lax.fori_loop(0, b, body, (m, l, acc))
        out = jnp.where(head, acc / l, out)
    o_ref[...] = out


def _prompt_attn(z):
    rows = z.shape[0]
    q0 = 3 * BRANCH_WIDTH // LANES
    k0 = 4 * BRANCH_WIDTH // LANES
    v0 = 5 * BRANCH_WIDTH // LANES
    return pl.pallas_call(
        _prompt_attn_kernel,
        grid=(BATCH, BRANCH_WIDTH // LANES, N_PROMPT_BLOCKS),
        in_specs=[
            pl.BlockSpec((MOBA_BLOCK, LANES), lambda n, hp, b: (n * N_PROMPT_BLOCKS + b, q0 + hp)),
            pl.BlockSpec((SEQ, LANES), lambda n, hp, b: (n, k0 + hp)),
            pl.BlockSpec((SEQ, LANES), lambda n, hp, b: (n, v0 + hp)),
        ],
        out_specs=pl.BlockSpec((MOBA_BLOCK, LANES), lambda n, hp, b: (n * N_PROMPT_BLOCKS + b, hp)),
        out_shape=jax.ShapeDtypeStruct((rows, BRANCH_WIDTH), F32),
        scratch_shapes=[pltpu.VMEM((SEQ, LANES), BF16), pltpu.VMEM((SEQ, LANES), BF16),
                        pltpu.VMEM((N_PROMPT_BLOCKS, LANES), F32)],
        compiler_params=_cparams(3),
        name="prompt_attn",
    )(z, z, z)


def _finish_kernel(x_ref, oab_ref, oc_ref, gates_ref, wb_ref, wo_ref, g_ref, o_ref):
    br_a = jnp.dot(oab_ref[:, 0:BRANCH_WIDTH].astype(BF16), wb_ref[0], preferred_element_type=F32)
    br_b = jnp.dot(oab_ref[:, BRANCH_WIDTH:2 * BRANCH_WIDTH].astype(BF16), wb_ref[1], preferred_element_type=F32)
    br_c = jnp.dot(oc_ref[...].astype(BF16), wb_ref[2], preferred_element_type=F32)
    mixed = (gates_ref[:, 0:D_MODEL] * br_a + gates_ref[:, D_MODEL:2 * D_MODEL] * br_b
             + gates_ref[:, 2 * D_MODEL:3 * D_MODEL] * br_c)
    mix = jnp.dot(mixed.astype(BF16), wo_ref[...], preferred_element_type=F32)
    o_ref[...] = x_ref[...] + _rms(mix, g_ref[...])


def _finish(x2d, oab, oc, z, w_branch_bf, w_out_bf, g_post_mix, tm):
    rows = x2d.shape[0]
    return pl.pallas_call(
        _finish_kernel,
        grid=(rows // tm,),
        in_specs=[
            pl.BlockSpec((tm, D_MODEL), lambda i: (i, 0)),
            pl.BlockSpec((tm, 2 * BRANCH_WIDTH), lambda i: (i, 0)),
            pl.BlockSpec((tm, BRANCH_WIDTH), lambda i: (i, 0)),
            pl.BlockSpec((tm, N_BRANCH * D_MODEL), lambda i: (i, 1)),
            pl.BlockSpec((N_BRANCH, BRANCH_WIDTH, D_MODEL), lambda i: (0, 0, 0)),
            pl.BlockSpec((D_MODEL, D_MODEL), lambda i: (0, 0)),
            pl.BlockSpec((1, D_MODEL), lambda i: (0, 0)),
        ],
        out_specs=pl.BlockSpec((tm, D_MODEL), lambda i: (i, 0)),
        out_shape=jax.ShapeDtypeStruct((rows, D_MODEL), F32),
        compiler_params=_cparams(1),
        name="finish",
    )(x2d, oab, oc, z, w_branch_bf, w_out_bf, g_post_mix)


def _ffn_kernel(x_ref, gpre_ref, wg_ref, wu_ref, wo_ref, gpost_ref, o_ref, xn_ref, acc_ref):
    f = pl.program_id(1)

    @pl.when(f == 0)
    def _():
        xn_ref[...] = _rms(x_ref[...], gpre_ref[...]).astype(BF16)
        acc_ref[...] = jnp.zeros_like(acc_ref)

    xn = xn_ref[...]
    gt = jnp.dot(xn, wg_ref[...], preferred_element_type=F32)
    up = jnp.dot(xn, wu_ref[...], preferred_element_type=F32)
    h = (gt * _sigmoid(gt)) * up
    acc_ref[...] += jnp.dot(h.astype(BF16), wo_ref[...], preferred_element_type=F32)

    @pl.when(f == pl.num_programs(1) - 1)
    def _():
        o_ref[...] = x_ref[...] + _rms(acc_ref[...], gpost_ref[...])


def _ffn(x2d, g_pre, w_ff_in_bf, w_ff_out_bf, g_post, tm):
    rows = x2d.shape[0]
    n_f = D_FF // FF_TILE
    return pl.pallas_call(
        _ffn_kernel,
        grid=(rows // tm, n_f),
        in_specs=[
            pl.BlockSpec((tm, D_MODEL), lambda i, f: (i, 0)),
            pl.BlockSpec((1, D_MODEL), lambda i, f: (0, 0)),
            pl.BlockSpec((D_MODEL, FF_TILE), lambda i, f: (0, f)),
            pl.BlockSpec((D_MODEL, FF_TILE), lambda i, f: (0, n_f + f)),
            pl.BlockSpec((FF_TILE, D_MODEL), lambda i, f: (f, 0)),
            pl.BlockSpec((1, D_MODEL), lambda i, f: (0, 0)),
        ],
        out_specs=pl.BlockSpec((tm, D_MODEL), lambda i, f: (i, 0)),
        out_shape=jax.ShapeDtypeStruct((rows, D_MODEL), F32),
        scratch_shapes=[pltpu.VMEM((tm, D_MODEL), BF16), pltpu.VMEM((tm, D_MODEL), F32)],
        compiler_params=_cparams(2),
        name="ffn",
    )(x2d, g_pre, w_ff_in_bf, w_ff_in_bf, w_ff_out_bf, g_post)


SAMPLE_ROWS = DEC_BATCH * DEC_SEQ


def _sample_mix_kernel(u_ref, vn_ref, p_ref, st_ref, wv_ref, bv_ref, wp_ref, sc_ref, oab_ref):
    def slab(ref, s):
        return ref[s * DEC_BATCH:(s + 1) * DEC_BATCH, :]

    for t in range(DEC_SEQ):
        s_t = bv_ref[t:t + 1, :]
        for sp in range(t + 1):
            s_t = s_t + wv_ref[t, sp:sp + 1, :] * slab(vn_ref, sp)
        oab_ref[t * DEC_BATCH:(t + 1) * DEC_BATCH, 0:BRANCH_WIDTH] = slab(u_ref, t) * s_t

    ext = [st_ref[r] for r in range(POOL_KEEP)] + [slab(p_ref, s) for s in range(DEC_SEQ)]
    lane = lax.broadcasted_iota(jnp.int32, (1, BRANCH_WIDTH), 1)
    d_rows = []
    for s in range(DEC_SEQ):
        end = POOL_KEEP + s + 1
        mean = jnp.zeros((DEC_BATCH, BRANCH_WIDTH), F32)
        run = jnp.zeros((DEC_BATCH, BRANCH_WIDTH), F32)
        taken = 0
        for gi, win in enumerate(POOL_WINDOWS):
            for r in range(end - win, end - taken):
                run = run + ext[r]
            taken = win
            mean = jnp.where((lane // POOL_GROUP_WIDTH) == gi, run / float(win), mean)
        d_rows.append(mean - ext[POOL_KEEP + s])
    d = jnp.concatenate(d_rows, axis=0).astype(BF16)
    for gi in range(len(POOL_WINDOWS)):
        cs = slice(gi * POOL_GROUP_WIDTH, (gi + 1) * POOL_GROUP_WIDTH)
        y = jnp.dot(d[:, cs], wp_ref[gi], preferred_element_type=F32)
        oab_ref[:, BRANCH_WIDTH + gi * POOL_GROUP_WIDTH:BRANCH_WIDTH + (gi + 1) * POOL_GROUP_WIDTH] = y * sc_ref[:, cs]


def _sample_mix(z, state_t, w_vec, b_vec, w_pool_bf, pool_scale):
    return pl.pallas_call(
        _sample_mix_kernel,
        grid=(1,),
        in_specs=[
            pl.BlockSpec((SAMPLE_ROWS, BRANCH_WIDTH), lambda i: (0, 0)),
            pl.BlockSpec((SAMPLE_ROWS, BRANCH_WIDTH), lambda i: (0, 1)),
            pl.BlockSpec((SAMPLE_ROWS, BRANCH_WIDTH), lambda i: (0, 2)),
            pl.BlockSpec((POOL_KEEP, DEC_BATCH, BRANCH_WIDTH), lambda i: (0, 0, 0)),
            pl.BlockSpec((DEC_SEQ, DEC_SEQ, BRANCH_WIDTH), lambda i: (0, 0, 0)),
            pl.BlockSpec((DEC_SEQ, BRANCH_WIDTH), lambda i: (0, 0)),
            pl.BlockSpec((len(POOL_WINDOWS), POOL_GROUP_WIDTH, POOL_GROUP_WIDTH), lambda i: (0, 0, 0)),
            pl.BlockSpec((1, BRANCH_WIDTH), lambda i: (0, 0)),
        ],
        out_specs=pl.BlockSpec((SAMPLE_ROWS, 2 * BRANCH_WIDTH), lambda i: (0, 0)),
        out_shape=jax.ShapeDtypeStruct((SAMPLE_ROWS, 2 * BRANCH_WIDTH), F32),
        compiler_params=_cparams(1),
        name="sample_mix",
    )(z, z, z, state_t, w_vec, b_vec, w_pool_bf, pool_scale)


KM_PAGES = 8


def _kmeans_kernel(pt_ref, *refs):
    del pt_ref
    page_refs, km_ref = refs[:KM_PAGES], refs[KM_PAGES]
    for jj in range(KM_PAGES // PAGES_PER_BLOCK):
        tot = jnp.sum(page_refs[PAGES_PER_BLOCK * jj][...], axis=1)
        for r in range(1, PAGES_PER_BLOCK):
            tot = tot + jnp.sum(page_refs[PAGES_PER_BLOCK * jj + r][...], axis=1)
        km_ref[0, jj] = tot * (1.0 / MOBA_BLOCK)


def _kmeans(cache_k, page_table, layer):
    n_pages = PAST_LEN // PAGE_SIZE
    blocks_per_step = KM_PAGES // PAGES_PER_BLOCK

    def page_spec(r):
        return pl.BlockSpec((None, None, N_HEADS, PAGE_SIZE, HEAD_DIM),
                            lambda n, c, pt: (layer, pt[n, c * KM_PAGES + r], 0, 0, 0))

    return pl.pallas_call(
        _kmeans_kernel,
        grid_spec=pltpu.PrefetchScalarGridSpec(
            num_scalar_prefetch=1,
            grid=(DEC_BATCH, n_pages // KM_PAGES),
            in_specs=[page_spec(r) for r in range(KM_PAGES)],
            out_specs=pl.BlockSpec((1, blocks_per_step, N_HEADS, HEAD_DIM), lambda n, c, pt: (n, c, 0, 0)),
        ),
        out_shape=jax.ShapeDtypeStruct((DEC_BATCH, N_PAST_BLOCKS, N_HEADS, HEAD_DIM), F32),
        compiler_params=_cparams(2),
        name="kmeans",
    )(page_table, *([cache_k] * KM_PAGES))


def _gate_topk_kernel(q_ref, km_ref, idx_ref):
    nt = (((1,), (1,)), ((), ()))
    lane_f = lax.broadcasted_iota(jnp.int32, (DEC_SEQ, N_PAST_BLOCKS), 1).astype(F32)
    for h in range(N_HEADS):
        gate = lax.dot_general(q_ref[0, h], km_ref[0, h], nt, precision=lax.Precision.HIGHEST,
                               preferred_element_type=F32)
        g = gate
        for r in range(MOBA_TOP_K):
            mx = jnp.max(g, axis=-1, keepdims=True)
            first = jnp.min(jnp.where(g == mx, lane_f, float(N_PAST_BLOCKS)), axis=-1, keepdims=True)
            idx_ref[0, h, :, r:r + 1] = first.astype(jnp.int32)
            g = jnp.where(lane_f == first, NEG, g)


def _gate_topk(q_nh, km_nh):
    return pl.pallas_call(
        _gate_topk_kernel,
        grid=(DEC_BATCH,),
        in_specs=[
            pl.BlockSpec((1, N_HEADS, DEC_SEQ, HEAD_DIM), lambda n: (n, 0, 0, 0)),
            pl.BlockSpec((1, N_HEADS, N_PAST_BLOCKS, HEAD_DIM), lambda n: (n, 0, 0, 0)),
        ],
        out_specs=pl.BlockSpec((1, N_HEADS, DEC_SEQ, MOBA_TOP_K), lambda n: (n, 0, 0, 0)),
        out_shape=jax.ShapeDtypeStruct((DEC_BATCH, N_HEADS, DEC_SEQ, MOBA_TOP_K), jnp.int32),
        compiler_params=_cparams(1),
        name="gate_topk",
    )(q_nh, km_nh)


N_SEL_PAGES = MOBA_TOP_K * PAGES_PER_BLOCK


def _sample_attn_kernel(pt_ref, idx_ref, q_ref, kn_ref, vn_ref, *refs):
    del pt_ref, idx_ref
    n_g = DEC_SEQ * N_SEL_PAGES
    k_refs, v_refs, o_ref = refs[:n_g], refs[n_g:2 * n_g], refs[2 * n_g]
    nt = (((1,), (1,)), ((), ()))
    scale = HEAD_DIM ** -0.5
    k_new = kn_ref[0, 0].astype(BF16)
    v_new = vn_ref[0, 0].astype(BF16)
    pos = lax.broadcasted_iota(jnp.int32, (1, DEC_SEQ), 1)
    for s in range(DEC_SEQ):
        q = (q_ref[0, 0, s:s + 1, :] * scale).astype(BF16)
        ks = jnp.concatenate([k_refs[s * N_SEL_PAGES + r][...] for r in range(N_SEL_PAGES)], axis=0).astype(BF16)
        vs = jnp.concatenate([v_refs[s * N_SEL_PAGES + r][...] for r in range(N_SEL_PAGES)], axis=0).astype(BF16)
        s_sel = lax.dot_general(q, ks, nt, preferred_element_type=F32)
        s_own = jnp.where(pos <= s, lax.dot_general(q, k_new, nt, preferred_element_type=F32), NEG)
        m = jnp.maximum(jnp.max(s_sel, axis=-1, keepdims=True), jnp.max(s_own, axis=-1, keepdims=True))
        p_sel = jnp.exp(s_sel - m)
        p_own = jnp.exp(s_own - m)
        l = jnp.sum(p_sel, axis=-1, keepdims=True) + jnp.sum(p_own, axis=-1, keepdims=True)
        o = (jnp.dot(p_sel.astype(BF16), vs, preferred_element_type=F32)
             + jnp.dot(p_own.astype(BF16), v_new, preferred_element_type=F32))
        o_ref[0, 0, s:s + 1, :] = o / l


def _sample_attn(q_nh, k_nh, v_nh, cache_k, cache_v, page_table, idx, layer):
    def new_spec():
        return pl.BlockSpec((1, 1, DEC_SEQ, HEAD_DIM), lambda n, h, pt, ix: (n, h, 0, 0))

    def page_spec(s, r):
        def index_map(n, h, pt, ix):
            blk = ix[((n * N_HEADS + h) * DEC_SEQ + s) * MOBA_TOP_K + r // PAGES_PER_BLOCK]
            return (layer, pt[n, blk * PAGES_PER_BLOCK + r % PAGES_PER_BLOCK], h, 0, 0)
        return pl.BlockSpec((None, None, None, PAGE_SIZE, HEAD_DIM), index_map)

    gathered = [page_spec(s, r) for s in range(DEC_SEQ) for r in range(N_SEL_PAGES)]
    return pl.pallas_call(
        _sample_attn_kernel,
        grid_spec=pltpu.PrefetchScalarGridSpec(
            num_scalar_prefetch=2,
            grid=(DEC_BATCH, N_HEADS),
            in_specs=[new_spec(), new_spec(), new_spec()] + gathered + gathered,
            out_specs=new_spec(),
        ),
        out_shape=jax.ShapeDtypeStruct((DEC_BATCH, N_HEADS, DEC_SEQ, HEAD_DIM), F32),
        compiler_params=_cparams(2),
        name="sample_attn",
    )(page_table, idx.reshape(-1), q_nh, k_nh, v_nh,
      *([cache_k] * len(gathered)), *([cache_v] * len(gathered)))


def _rope_tables(pos):
    half = HEAD_DIM // 2
    inv = ROPE_THETA ** (-jnp.arange(half, dtype=F32) / half)
    ang = pos.astype(F32)[:, None] * inv[None, :]
    cos, sin = jnp.cos(ang), jnp.sin(ang)
    reps = LANES // half
    cos_t = jnp.tile(cos, (1, reps))
    sin_t = jnp.concatenate([-sin, sin] * (reps // 2), axis=1)
    return cos_t, sin_t


def _to_pages(a2d):
    return a2d.reshape(BATCH, SEQ // PAGE_SIZE, PAGE_SIZE, N_HEADS, HEAD_DIM).transpose(0, 1, 3, 2, 4)


def _sample_heads(a2d):
    return a2d.reshape(DEC_SEQ, DEC_BATCH, N_HEADS, HEAD_DIM).transpose(1, 2, 0, 3)


def _sample_rows(a2d):
    return a2d.reshape(DEC_SEQ, DEC_BATCH, a2d.shape[-1]).transpose(1, 0, 2)


def kernel(x_prompt, x_sample, cache_k, cache_v, state_pool, page_table, w_in, g_v, w_s, b_s, w_pool, pool_scale, w_branch, w_out, g_pre_mix, g_post_mix, g_pre_ffn, g_post_ffn, w_ff_in, w_ff_out):
    cos_p, sin_p = _rope_tables(jnp.arange(SEQ))
    cos_s, sin_s = _rope_tables(PAST_LEN + jnp.arange(SAMPLE_ROWS) // DEC_BATCH)

    xp = x_prompt.reshape(BATCH * SEQ, D_MODEL)
    xs = x_sample.transpose(1, 0, 2).reshape(SAMPLE_ROWS, D_MODEL)
    page_table = page_table.astype(jnp.int32)

    kp_l, vp_l, poolp_l, ks_l, vs_l, pools_l, cvs_l = [], [], [], [], [], [], []
    for l in range(DEPTH):
        w_in_bf = w_in[l].astype(BF16)
        w_pool_bf = w_pool[l].astype(BF16)
        w_branch_bf = w_branch[l].astype(BF16)
        w_out_bf = w_out[l].astype(BF16)
        w_ff_in_bf = w_ff_in[l].astype(BF16)
        w_ff_out_bf = w_ff_out[l].astype(BF16)
        row = lambda a: a[l].reshape(1, -1)
        b_s_t = b_s[l].T

        zp = _inproj(xp, row(g_pre_mix), w_in_bf, row(g_v), cos_p, sin_p, tm=1024)
        oab = _prompt_mix(zp, w_s[l], b_s_t, w_pool_bf, row(pool_scale))
        oc = _prompt_attn(zp)
        x1 = _finish(xp, oab, oc, zp, w_branch_bf, w_out_bf, row(g_post_mix), tm=256)
        xp = _ffn(x1, row(g_pre_ffn), w_ff_in_bf, w_ff_out_bf, row(g_post_ffn), tm=512)
        kp_l.append(_to_pages(zp[:, 4 * BRANCH_WIDTH:5 * BRANCH_WIDTH]))
        vp_l.append(_to_pages(zp[:, 5 * BRANCH_WIDTH:6 * BRANCH_WIDTH]))
        poolp_l.append(zp[:, 2 * BRANCH_WIDTH:3 * BRANCH_WIDTH]
                       .reshape(BATCH, SEQ, BRANCH_WIDTH)[:, SEQ - POOL_KEEP:])

        zs = _inproj(xs, row(g_pre_mix), w_in_bf, row(g_v), cos_s, sin_s, tm=SAMPLE_ROWS)
        w_vec = jnp.repeat(w_s[l][:, :DEC_SEQ, :DEC_SEQ].transpose(1, 2, 0), CM_GROUP_WIDTH, axis=-1)
        b_vec = jnp.repeat(b_s[l][:, :DEC_SEQ].T, CM_GROUP_WIDTH, axis=-1)
        state_t = state_pool[l].transpose(1, 0, 2)
        oab_s = _sample_mix(zs, state_t, w_vec, b_vec, w_pool_bf, row(pool_scale))
        q_nh = _sample_heads(zs[:, 3 * BRANCH_WIDTH:4 * BRANCH_WIDTH])
        k_nh = _sample_heads(zs[:, 4 * BRANCH_WIDTH:5 * BRANCH_WIDTH])
        v_nh = _sample_heads(zs[:, 5 * BRANCH_WIDTH:6 * BRANCH_WIDTH])
        km = _kmeans(cache_k, page_table, l)
        idx = _gate_topk(q_nh, km.transpose(0, 2, 1, 3))
        oc_nh = _sample_attn(q_nh, k_nh, v_nh, cache_k, cache_v, page_table, idx, l)
        oc_s = oc_nh.transpose(2, 0, 1, 3).reshape(SAMPLE_ROWS, BRANCH_WIDTH)
        x1s = _finish(xs, oab_s, oc_s, zs, w_branch_bf, w_out_bf, row(g_post_mix), tm=SAMPLE_ROWS)
        xs = _ffn(x1s, row(g_pre_ffn), w_ff_in_bf, w_ff_out_bf, row(g_post_ffn), tm=SAMPLE_ROWS)
        ks_l.append(k_nh)
        vs_l.append(v_nh)
        p_s = _sample_rows(zs[:, 2 * BRANCH_WIDTH:3 * BRANCH_WIDTH])
        pools_l.append(jnp.concatenate([state_pool[l][:, DEC_SEQ:], p_s], axis=1))
        cvs_l.append(_sample_rows(zs[:, BRANCH_WIDTH:2 * BRANCH_WIDTH]))

    y_prompt = xp.reshape(BATCH, SEQ, D_MODEL)
    y_sample = _sample_rows(xs)
    return (y_prompt, y_sample, jnp.stack(kp_l), jnp.stack(vp_l), jnp.stack(poolp_l),
            jnp.stack(ks_l), jnp.stack(vs_l), jnp.stack(pools_l), jnp.stack(cvs_l))
```

```python
import functools

import numpy as np
import jax
import jax.numpy as jnp
from jax import lax
from jax.experimental import pallas as pl
from jax.experimental.pallas import tpu as pltpu

D_MODEL = 1024
BATCH = 4
SEQ = 4096
DEPTH = 2
DEC_BATCH = 32
DEC_SEQ = 4
PAST_LEN = 16384
PAGE_SIZE = 128

BRANCH_WIDTH = D_MODEL // 2
HEAD_DIM = 64
N_HEADS = BRANCH_WIDTH // HEAD_DIM
MOBA_BLOCK = 256
MOBA_TOP_K = 3
ROPE_THETA = 10000.0
CM_CHUNK = 128
CM_GROUPS = 8
CM_GROUP_WIDTH = BRANCH_WIDTH // CM_GROUPS
POOL_WINDOWS = (2, 4, 8, 16)
POOL_GROUP_WIDTH = BRANCH_WIDTH // len(POOL_WINDOWS)
POOL_KEEP = max(POOL_WINDOWS) - 1
N_BRANCH = 3
D_FF = -(-8 * D_MODEL // (3 * 256)) * 256
IN_WIDTH = 6 * BRANCH_WIDTH + N_BRANCH * D_MODEL
NORM_EPS = 1e-6
NEG = -1e30

LANES = 128
COL_TILE = BRANCH_WIDTH
N_COL_TILES = IN_WIDTH // COL_TILE
GATE_COL0 = 6 * BRANCH_WIDTH
PAGES_PER_BLOCK = MOBA_BLOCK // PAGE_SIZE
N_PAST_BLOCKS = PAST_LEN // MOBA_BLOCK
N_PROMPT_BLOCKS = SEQ // MOBA_BLOCK
FF_TILE = D_FF // 2
VMEM_LIMIT = 56 * 1024 * 1024

F32 = jnp.float32
BF16 = jnp.bfloat16


def _cparams(n_axes):
    return pltpu.CompilerParams(dimension_semantics=("arbitrary",) * n_axes,
                                vmem_limit_bytes=VMEM_LIMIT)


def _rms(x, g):
    return (x * lax.rsqrt(jnp.mean(x * x, axis=-1, keepdims=True) + NORM_EPS)) * g


def _sigmoid(x):
    return 1.0 / (1.0 + jnp.exp(-x))


def _inproj_kernel(x_ref, g_ref, w_ref, gv_ref, cos_ref, sin_ref, z_ref, xn_ref):
    j = pl.program_id(1)

    @pl.when(j == 0)
    def _():
        xn_ref[...] = _rms(x_ref[...], g_ref[...]).astype(BF16)

    acc = jnp.dot(xn_ref[...], w_ref[...], preferred_element_type=F32)

    @pl.when(j == 0)
    def _():
        z_ref[...] = jax.nn.gelu(acc)

    @pl.when(j == 1)
    def _():
        v = jax.nn.gelu(acc)
        vc = v - jnp.mean(v, axis=-1, keepdims=True)
        y = vc * lax.rsqrt(jnp.mean(vc * vc, axis=-1, keepdims=True) + NORM_EPS)
        z_ref[...] = y * gv_ref[...]

    @pl.when((j == 2) | (j == 5))
    def _():
        z_ref[...] = acc

    @pl.when((j == 3) | (j == 4))
    def _():
        lane = lax.broadcasted_iota(jnp.int32, (1, LANES), 1)
        first_half = (lane % HEAD_DIM) < (HEAD_DIM // 2)
        cos = cos_ref[...]
        sin = sin_ref[...]
        for c in range(COL_TILE // LANES):
            a = acc[:, c * LANES:(c + 1) * LANES]
            partner = jnp.where(first_half,
                                pltpu.roll(a, LANES - HEAD_DIM // 2, axis=1),
                                pltpu.roll(a, HEAD_DIM // 2, axis=1))
            z_ref[:, c * LANES:(c + 1) * LANES] = a * cos + partner * sin

    @pl.when(j >= GATE_COL0 // COL_TILE)
    def _():
        z_ref[...] = _sigmoid(acc)


def _inproj(x2d, g_pre, w_in_bf, g_v, cos_t, sin_t, tm):
    rows = x2d.shape[0]
    t_tiles = cos_t.shape[0] // tm
    return pl.pallas_call(
        _inproj_kernel,
        grid=(rows // tm, N_COL_TILES),
        in_specs=[
            pl.BlockSpec((tm, D_MODEL), lambda i, j: (i, 0)),
            pl.BlockSpec((1, D_MODEL), lambda i, j: (0, 0)),
            pl.BlockSpec((D_MODEL, COL_TILE), lambda i, j: (0, j)),
            pl.BlockSpec((1, COL_TILE), lambda i, j: (0, 0)),
            pl.BlockSpec((tm, LANES), lambda i, j: (i % t_tiles, 0)),
            pl.BlockSpec((tm, LANES), lambda i, j: (i % t_tiles, 0)),
        ],
        out_specs=pl.BlockSpec((tm, COL_TILE), lambda i, j: (i, j)),
        out_shape=jax.ShapeDtypeStruct((rows, IN_WIDTH), F32),
        scratch_shapes=[pltpu.VMEM((tm, D_MODEL), BF16)],
        compiler_params=_cparams(2),
        name="inproj",
    )(x2d, g_pre, w_in_bf, g_v, cos_t, sin_t)


MIX_TILE = 2 * CM_CHUNK
HALO = 16


def _prompt_mix_kernel(u_ref, vn_ref, p_ref, prev_ref, ws_ref, bs_ref, wp_ref, sc_ref,
                       oab_ref, ext_ref):
    i = pl.program_id(0)
    tiles_per_seq = SEQ // MIX_TILE
    first = (i % tiles_per_seq) == 0

    row = lax.broadcasted_iota(jnp.int32, (CM_CHUNK, CM_CHUNK), 0)
    col = lax.broadcasted_iota(jnp.int32, (CM_CHUNK, CM_CHUNK), 1)
    tri = row >= col
    lane = lax.broadcasted_iota(jnp.int32, (1, LANES), 1)
    lo_half = lane < CM_GROUP_WIDTH
    w_tri = [jnp.where(tri, ws_ref[g], 0.0).astype(BF16) for g in range(CM_GROUPS)]
    for c in range(MIX_TILE // CM_CHUNK):
        rs = slice(c * CM_CHUNK, (c + 1) * CM_CHUNK)
        for pr in range(CM_GROUPS // 2):
            cs = slice(pr * LANES, (pr + 1) * LANES)
            vn = vn_ref[rs, cs]
            v_lo = jnp.where(lo_half, vn, 0.0).astype(BF16)
            v_hi = jnp.where(lo_half, 0.0, vn).astype(BF16)
            s = (jnp.dot(w_tri[2 * pr], v_lo, preferred_element_type=F32)
                 + jnp.dot(w_tri[2 * pr + 1], v_hi, preferred_element_type=F32))
            bias = jnp.where(lo_half, bs_ref[:, 2 * pr:2 * pr + 1], bs_ref[:, 2 * pr + 1:2 * pr + 2])
            oab_ref[rs, cs] = u_ref[rs, cs] * (s + bias)

    ext_ref[0:HALO, :] = jnp.where(first, 0.0, prev_ref[...])
    ext_ref[HALO:HALO + MIX_TILE, :] = p_ref[...]
    t_in_seq = (i % tiles_per_seq) * MIX_TILE + lax.broadcasted_iota(jnp.int32, (MIX_TILE, 1), 0)
    for gi, win in enumerate(POOL_WINDOWS):
        cs = slice(gi * POOL_GROUP_WIDTH, (gi + 1) * POOL_GROUP_WIDTH)
        tot = ext_ref[HALO:HALO + MIX_TILE, cs]
        for k in range(1, win):
            tot = tot + ext_ref[HALO - k:HALO - k + MIX_TILE, cs]
        cnt = jnp.minimum(t_in_seq + 1, win).astype(F32)
        d = tot / cnt - p_ref[:, cs]
        y = jnp.dot(d.astype(BF16), wp_ref[gi], preferred_element_type=F32)
        oab_ref[:, BRANCH_WIDTH + gi * POOL_GROUP_WIDTH:BRANCH_WIDTH + (gi + 1) * POOL_GROUP_WIDTH] = y * sc_ref[:, cs]


def _prompt_mix(z, w_s, b_s_t, w_pool_bf, pool_scale):
    rows = z.shape[0]
    halo_blocks = MIX_TILE // HALO
    return pl.pallas_call(
        _prompt_mix_kernel,
        grid=(rows // MIX_TILE,),
        in_specs=[
            pl.BlockSpec((MIX_TILE, BRANCH_WIDTH), lambda i: (i, 0)),
            pl.BlockSpec((MIX_TILE, BRANCH_WIDTH), lambda i: (i, 1)),
            pl.BlockSpec((MIX_TILE, BRANCH_WIDTH), lambda i: (i, 2)),
            pl.BlockSpec((HALO, BRANCH_WIDTH), lambda i: (jnp.maximum(i * halo_blocks - 1, 0), 2)),
            pl.BlockSpec((CM_GROUPS, CM_CHUNK, CM_CHUNK), lambda i: (0, 0, 0)),
            pl.BlockSpec((CM_CHUNK, CM_GROUPS), lambda i: (0, 0)),
            pl.BlockSpec((len(POOL_WINDOWS), POOL_GROUP_WIDTH, POOL_GROUP_WIDTH), lambda i: (0, 0, 0)),
            pl.BlockSpec((1, BRANCH_WIDTH), lambda i: (0, 0)),
        ],
        out_specs=pl.BlockSpec((MIX_TILE, 2 * BRANCH_WIDTH), lambda i: (i, 0)),
        out_shape=jax.ShapeDtypeStruct((rows, 2 * BRANCH_WIDTH), F32),
        scratch_shapes=[pltpu.VMEM((HALO + MIX_TILE, BRANCH_WIDTH), F32)],
        compiler_params=_cparams(1),
        name="prompt_mix",
    )(z, z, z, z, w_s, b_s_t, w_pool_bf, pool_scale)


HEADS_PER_TILE = LANES // HEAD_DIM


def _top_k_rows(gate_t, blk_f, n_valid):
    n_blk = gate_t.shape[0]
    g = jnp.where(blk_f < n_valid, gate_t, NEG)
    sel = jnp.zeros(gate_t.shape, F32)
    for _ in range(MOBA_TOP_K):
        mx = jnp.max(g, axis=0, keepdims=True)
        first = jnp.min(jnp.where(g == mx, blk_f, float(n_blk)), axis=0, keepdims=True)
        pick = jnp.where(blk_f == first, jnp.where(mx > NEG, 1.0, 0.0), 0.0)
        sel = jnp.maximum(sel, pick)
        g = jnp.where(pick > 0.0, NEG, g)
    return sel


ATTN_GROUP = 4
OWN_SLOT = N_PROMPT_BLOCKS


def _prompt_attn_kernel(q_ref, k_ref, v_ref, o_ref, kbf_ref, vt_ref, km_ref, sel_ref, s_ref, acc_ref):
    b = pl.program_id(2)

    @pl.when(b == 0)
    def _():
        drow = lax.broadcasted_iota(jnp.int32, (LANES, MOBA_BLOCK), 0)
        for j in range(N_PROMPT_BLOCKS):
            kj = k_ref[j * MOBA_BLOCK:(j + 1) * MOBA_BLOCK, :]
            kbf_ref[j] = kj.astype(BF16)
            km_ref[j:j + 1, :] = jnp.mean(kj, axis=0, keepdims=True)
            vt = v_ref[j * MOBA_BLOCK:(j + 1) * MOBA_BLOCK, :].T
            for hh in range(HEADS_PER_TILE):
                vt_ref[hh, j] = jnp.where((drow // HEAD_DIM) == hh, vt, 1.0).astype(BF16)

    q = q_ref[...] * (HEAD_DIM ** -0.5)
    lane = lax.broadcasted_iota(jnp.int32, (1, LANES), 1)
    blk_f = lax.broadcasted_iota(jnp.int32, (N_PROMPT_BLOCKS, MOBA_BLOCK), 0).astype(F32)
    key_i = lax.broadcasted_iota(jnp.int32, (MOBA_BLOCK, MOBA_BLOCK), 0)
    qry_i = lax.broadcasted_iota(jnp.int32, (MOBA_BLOCK, MOBA_BLOCK), 1)
    nt = (((1,), (1,)), ((), ()))
    k_own = kbf_ref[b]
    b_f = b.astype(F32)
    n_groups = lax.shift_right_logical(b + (ATTN_GROUP - 1), ATTN_GROUP.bit_length() - 1)

    q_bf, m0 = [], []
    for hh in range(HEADS_PER_TILE):
        qh = jnp.where((lane // HEAD_DIM) == hh, q, 0.0)
        q_bf.append(qh.astype(BF16))
        gate_t = lax.dot_general(km_ref[...], qh, nt, precision=lax.Precision.HIGHEST,
                                 preferred_element_type=F32)
        sel_ref[hh] = _top_k_rows(gate_t, blk_f, b_f)
        s = lax.dot_general(k_own, q_bf[hh], nt, preferred_element_type=F32)
        s = jnp.where(key_i <= qry_i, s, NEG)
        s_ref[hh, OWN_SLOT] = s
        m0.append(jnp.max(s, axis=0, keepdims=True))

    def score_group(g, ms):
        ms = list(ms)
        for t in range(ATTN_GROUP):
            j = g * ATTN_GROUP + t
            kj = kbf_ref[j]
            for hh in range(HEADS_PER_TILE):
                s = lax.dot_general(kj, q_bf[hh], nt, preferred_element_type=F32)
                s = jnp.where(sel_ref[hh, pl.ds(j, 1), :] > 0.0, s, NEG)
                s_ref[hh, j] = s
                ms[hh] = jnp.maximum(ms[hh], jnp.max(s, axis=0, keepdims=True))
        return tuple(ms)

    ms = lax.fori_loop(0, n_groups, score_group, tuple(m0))

    for hh in range(HEADS_PER_TILE):
        p = jnp.exp(s_ref[hh, OWN_SLOT] - ms[hh]).astype(BF16)
        acc_ref[hh] = jnp.dot(vt_ref[hh, b], p, preferred_element_type=F32)

    def value_group(g, carry):
        for hh in range(HEADS_PER_TILE):
            acc = acc_ref[hh]
            for t in range(ATTN_GROUP):
                j = g * ATTN_GROUP + t
                p = jnp.exp(s_ref[hh, j] - ms[hh]).astype(BF16)
                acc = acc + jnp.dot(vt_ref[hh, j], p, preferred_element_type=F32)
            acc_ref[hh] = acc
        return carry

    lax.fori_loop(0, n_groups, value_group, 0)

    drow = lax.broadcasted_iota(jnp.int32, (LANES, MOBA_BLOCK), 0)
    a0 = acc_ref[0]
    a1 = acc_ref[1]
    out_t = jnp.where(drow < HEAD_DIM, a0 / a0[HEAD_DIM:HEAD_DIM + 1, :], a1 / a1[0:1, :])
    o_ref[...] = out_t.T


def _prompt_attn(z):
    rows = z.shape[0]
    q0 = 3 * BRANCH_WIDTH // LANES
    k0 = 4 * BRANCH_WIDTH // LANES
    v0 = 5 * BRANCH_WIDTH // LANES
    assert HEADS_PER_TILE == 2 and N_PROMPT_BLOCKS % ATTN_GROUP == 0
    return pl.pallas_call(
        _prompt_attn_kernel,
        grid=(BATCH, BRANCH_WIDTH // LANES, N_PROMPT_BLOCKS),
        in_specs=[
            pl.BlockSpec((MOBA_BLOCK, LANES), lambda n, hp, b: (n * N_PROMPT_BLOCKS + b, q0 + hp)),
            pl.BlockSpec((SEQ, LANES), lambda n, hp, b: (n, k0 + hp)),
            pl.BlockSpec((SEQ, LANES), lambda n, hp, b: (n, v0 + hp)),
        ],
        out_specs=pl.BlockSpec((MOBA_BLOCK, LANES), lambda n, hp, b: (n * N_PROMPT_BLOCKS + b, hp)),
        out_shape=jax.ShapeDtypeStruct((rows, BRANCH_WIDTH), F32),
        scratch_shapes=[pltpu.VMEM((N_PROMPT_BLOCKS, MOBA_BLOCK, LANES), BF16),
                        pltpu.VMEM((HEADS_PER_TILE, N_PROMPT_BLOCKS, LANES, MOBA_BLOCK), BF16),
                        pltpu.VMEM((N_PROMPT_BLOCKS, LANES), F32),
                        pltpu.VMEM((HEADS_PER_TILE, N_PROMPT_BLOCKS, MOBA_BLOCK), F32),
                        pltpu.VMEM((HEADS_PER_TILE, N_PROMPT_BLOCKS + 1, MOBA_BLOCK, MOBA_BLOCK), F32),
                        pltpu.VMEM((HEADS_PER_TILE, LANES, MOBA_BLOCK), F32)],
        compiler_params=_cparams(3),
        name="prompt_attn",
    )(z, z, z)


def _finish_kernel(x_ref, oab_ref, oc_ref, gates_ref, wb_ref, wo_ref, g_ref, o_ref):
    br_a = jnp.dot(oab_ref[:, 0:BRANCH_WIDTH].astype(BF16), wb_ref[0], preferred_element_type=F32)
    br_b = jnp.dot(oab_ref[:, BRANCH_WIDTH:2 * BRANCH_WIDTH].astype(BF16), wb_ref[1], preferred_element_type=F32)
    br_c = jnp.dot(oc_ref[...].astype(BF16), wb_ref[2], preferred_element_type=F32)
    mixed = (gates_ref[:, 0:D_MODEL] * br_a + gates_ref[:, D_MODEL:2 * D_MODEL] * br_b
             + gates_ref[:, 2 * D_MODEL:3 * D_MODEL] * br_c)
    mix = jnp.dot(mixed.astype(BF16), wo_ref[...], preferred_element_type=F32)
    o_ref[...] = x_ref[...] + _rms(mix, g_ref[...])


def _finish(x2d, oab, oc, z, w_branch_bf, w_out_bf, g_post_mix, tm):
    rows = x2d.shape[0]
    return pl.pallas_call(
        _finish_kernel,
        grid=(rows // tm,),
        in_specs=[
            pl.BlockSpec((tm, D_MODEL), lambda i: (i, 0)),
            pl.BlockSpec((tm, 2 * BRANCH_WIDTH), lambda i: (i, 0)),
            pl.BlockSpec((tm, BRANCH_WIDTH), lambda i: (i, 0)),
            pl.BlockSpec((tm, N_BRANCH * D_MODEL), lambda i: (i, 1)),
            pl.BlockSpec((N_BRANCH, BRANCH_WIDTH, D_MODEL), lambda i: (0, 0, 0)),
            pl.BlockSpec((D_MODEL, D_MODEL), lambda i: (0, 0)),
            pl.BlockSpec((1, D_MODEL), lambda i: (0, 0)),
        ],
        out_specs=pl.BlockSpec((tm, D_MODEL), lambda i: (i, 0)),
        out_shape=jax.ShapeDtypeStruct((rows, D_MODEL), F32),
        compiler_params=_cparams(1),
        name="finish",
    )(x2d, oab, oc, z, w_branch_bf, w_out_bf, g_post_mix)


def _ffn_kernel(x_ref, gpre_ref, wg_ref, wu_ref, wo_ref, gpost_ref, o_ref, xn_ref, acc_ref):
    f = pl.program_id(1)

    @pl.when(f == 0)
    def _():
        xn_ref[...] = _rms(x_ref[...], gpre_ref[...]).astype(BF16)
        acc_ref[...] = jnp.zeros_like(acc_ref)

    xn = xn_ref[...]
    gt = jnp.dot(xn, wg_ref[...], preferred_element_type=F32)
    up = jnp.dot(xn, wu_ref[...], preferred_element_type=F32)
    h = (gt * _sigmoid(gt)) * up
    acc_ref[...] += jnp.dot(h.astype(BF16), wo_ref[...], preferred_element_type=F32)

    @pl.when(f == pl.num_programs(1) - 1)
    def _():
        o_ref[...] = x_ref[...] + _rms(acc_ref[...], gpost_ref[...])


def _ffn(x2d, g_pre, w_ff_in_bf, w_ff_out_bf, g_post, tm):
    rows = x2d.shape[0]
    n_f = D_FF // FF_TILE
    return pl.pallas_call(
        _ffn_kernel,
        grid=(rows // tm, n_f),
        in_specs=[
            pl.BlockSpec((tm, D_MODEL), lambda i, f: (i, 0)),
            pl.BlockSpec((1, D_MODEL), lambda i, f: (0, 0)),
            pl.BlockSpec((D_MODEL, FF_TILE), lambda i, f: (0, f)),
            pl.BlockSpec((D_MODEL, FF_TILE), lambda i, f: (0, n_f + f)),
            pl.BlockSpec((FF_TILE, D_MODEL), lambda i, f: (f, 0)),
            pl.BlockSpec((1, D_MODEL), lambda i, f: (0, 0)),
        ],
        out_specs=pl.BlockSpec((tm, D_MODEL), lambda i, f: (i, 0)),
        out_shape=jax.ShapeDtypeStruct((rows, D_MODEL), F32),
        scratch_shapes=[pltpu.VMEM((tm, D_MODEL), BF16), pltpu.VMEM((tm, D_MODEL), F32)],
        compiler_params=_cparams(2),
        name="ffn",
    )(x2d, g_pre, w_ff_in_bf, w_ff_in_bf, w_ff_out_bf, g_post)


SAMPLE_ROWS = DEC_BATCH * DEC_SEQ


def _sample_mix_kernel(u_ref, vn_ref, p_ref, st_ref, wv_ref, bv_ref, wp_ref, sc_ref, oab_ref):
    def slab(ref, s):
        return ref[s * DEC_BATCH:(s + 1) * DEC_BATCH, :]

    for t in range(DEC_SEQ):
        s_t = bv_ref[t:t + 1, :]
        for sp in range(t + 1):
            s_t = s_t + wv_ref[t, sp:sp + 1, :] * slab(vn_ref, sp)
        oab_ref[t * DEC_BATCH:(t + 1) * DEC_BATCH, 0:BRANCH_WIDTH] = slab(u_ref, t) * s_t

    ext = [st_ref[r] for r in range(POOL_KEEP)] + [slab(p_ref, s) for s in range(DEC_SEQ)]
    lane = lax.broadcasted_iota(jnp.int32, (1, BRANCH_WIDTH), 1)
    d_rows = []
    for s in range(DEC_SEQ):
        end = POOL_KEEP + s + 1
        mean = jnp.zeros((DEC_BATCH, BRANCH_WIDTH), F32)
        run = jnp.zeros((DEC_BATCH, BRANCH_WIDTH), F32)
        taken = 0
        for gi, win in enumerate(POOL_WINDOWS):
            for r in range(end - win, end - taken):
                run = run + ext[r]
            taken = win
            mean = jnp.where((lane // POOL_GROUP_WIDTH) == gi, run / float(win), mean)
        d_rows.append(mean - ext[POOL_KEEP + s])
    d = jnp.concatenate(d_rows, axis=0).astype(BF16)
    for gi in range(len(POOL_WINDOWS)):
        cs = slice(gi * POOL_GROUP_WIDTH, (gi + 1) * POOL_GROUP_WIDTH)
        y = jnp.dot(d[:, cs], wp_ref[gi], preferred_element_type=F32)
        oab_ref[:, BRANCH_WIDTH + gi * POOL_GROUP_WIDTH:BRANCH_WIDTH + (gi + 1) * POOL_GROUP_WIDTH] = y * sc_ref[:, cs]


def _sample_mix(z, state_t, w_vec, b_vec, w_pool_bf, pool_scale):
    return pl.pallas_call(
        _sample_mix_kernel,
        grid=(1,),
        in_specs=[
            pl.BlockSpec((SAMPLE_ROWS, BRANCH_WIDTH), lambda i: (0, 0)),
            pl.BlockSpec((SAMPLE_ROWS, BRANCH_WIDTH), lambda i: (0, 1)),
            pl.BlockSpec((SAMPLE_ROWS, BRANCH_WIDTH), lambda i: (0, 2)),
            pl.BlockSpec((POOL_KEEP, DEC_BATCH, BRANCH_WIDTH), lambda i: (0, 0, 0)),
            pl.BlockSpec((DEC_SEQ, DEC_SEQ, BRANCH_WIDTH), lambda i: (0, 0, 0)),
            pl.BlockSpec((DEC_SEQ, BRANCH_WIDTH), lambda i: (0, 0)),
            pl.BlockSpec((len(POOL_WINDOWS), POOL_GROUP_WIDTH, POOL_GROUP_WIDTH), lambda i: (0, 0, 0)),
            pl.BlockSpec((1, BRANCH_WIDTH), lambda i: (0, 0)),
        ],
        out_specs=pl.BlockSpec((SAMPLE_ROWS, 2 * BRANCH_WIDTH), lambda i: (0, 0)),
        out_shape=jax.ShapeDtypeStruct((SAMPLE_ROWS, 2 * BRANCH_WIDTH), F32),
        compiler_params=_cparams(1),
        name="sample_mix",
    )(z, z, z, state_t, w_vec, b_vec, w_pool_bf, pool_scale)


KM_PAGES = 16


def _kmeans_kernel(pt_ref, *refs):
    del pt_ref
    page_refs, km_ref = refs[:KM_PAGES], refs[KM_PAGES]
    c = pl.program_id(1)

    @pl.when(c == 0)
    def _():
        km_ref[...] = jnp.zeros_like(km_ref)

    blocks_per_step = KM_PAGES // PAGES_PER_BLOCK
    blk = lax.broadcasted_iota(jnp.int32, (1, 1, N_PAST_BLOCKS), 2)
    acc = km_ref[0]
    for jj in range(blocks_per_step):
        tot = page_refs[PAGES_PER_BLOCK * jj][...]
        for r in range(1, PAGES_PER_BLOCK):
            tot = tot + page_refs[PAGES_PER_BLOCK * jj + r][...]
        mean = jnp.sum(tot, axis=-1, keepdims=True) * (1.0 / MOBA_BLOCK)
        acc = jnp.where(blk == c * blocks_per_step + jj, mean, acc)
    km_ref[0] = acc


def _kmeans(cache_kt, page_table, layer):
    n_pages = PAST_LEN // PAGE_SIZE

    def page_spec(r):
        return pl.BlockSpec((None, None, N_HEADS, HEAD_DIM, PAGE_SIZE),
                            lambda n, c, pt: (layer, pt[n, c * KM_PAGES + r], 0, 0, 0))

    return pl.pallas_call(
        _kmeans_kernel,
        grid_spec=pltpu.PrefetchScalarGridSpec(
            num_scalar_prefetch=1,
            grid=(DEC_BATCH, n_pages // KM_PAGES),
            in_specs=[page_spec(r) for r in range(KM_PAGES)],
            out_specs=pl.BlockSpec((1, N_HEADS, HEAD_DIM, N_PAST_BLOCKS), lambda n, c, pt: (n, 0, 0, 0)),
        ),
        out_shape=jax.ShapeDtypeStruct((DEC_BATCH, N_HEADS, HEAD_DIM, N_PAST_BLOCKS), F32),
        compiler_params=_cparams(2),
        name="kmeans",
    )(page_table, *([cache_kt] * KM_PAGES))


def _gate_topk_kernel(q_ref, km_ref, idx_ref):
    lane_f = lax.broadcasted_iota(jnp.int32, (DEC_SEQ, N_PAST_BLOCKS), 1).astype(F32)
    for h in range(N_HEADS):
        gate = jnp.dot(q_ref[0, h], km_ref[0, h], precision=lax.Precision.HIGHEST,
                       preferred_element_type=F32)
        g = gate
        for r in range(MOBA_TOP_K):
            mx = jnp.max(g, axis=-1, keepdims=True)
            first = jnp.min(jnp.where(g == mx, lane_f, float(N_PAST_BLOCKS)), axis=-1, keepdims=True)
            idx_ref[0, h, :, r:r + 1] = first.astype(jnp.int32)
            g = jnp.where(lane_f == first, NEG, g)


def _gate_topk(q_nh, km_nh):
    return pl.pallas_call(
        _gate_topk_kernel,
        grid=(DEC_BATCH,),
        in_specs=[
            pl.BlockSpec((1, N_HEADS, DEC_SEQ, HEAD_DIM), lambda n: (n, 0, 0, 0)),
            pl.BlockSpec((1, N_HEADS, HEAD_DIM, N_PAST_BLOCKS), lambda n: (n, 0, 0, 0)),
        ],
        out_specs=pl.BlockSpec((1, N_HEADS, DEC_SEQ, MOBA_TOP_K), lambda n: (n, 0, 0, 0)),
        out_shape=jax.ShapeDtypeStruct((DEC_BATCH, N_HEADS, DEC_SEQ, MOBA_TOP_K), jnp.int32),
        compiler_params=_cparams(1),
        name="gate_topk",
    )(q_nh, km_nh)


N_SEL_PAGES = MOBA_TOP_K * PAGES_PER_BLOCK


def _sample_attn_kernel(pt_ref, idx_ref, q_ref, kn_ref, vn_ref, *refs):
    del pt_ref, idx_ref
    n_g = DEC_SEQ * N_SEL_PAGES
    k_refs, v_refs, o_ref = refs[:n_g], refs[n_g:2 * n_g], refs[2 * n_g]
    nt = (((1,), (1,)), ((), ()))
    scale = HEAD_DIM ** -0.5
    k_new = kn_ref[0, 0].astype(BF16)
    v_new = vn_ref[0, 0].astype(BF16)
    pos = lax.broadcasted_iota(jnp.int32, (1, DEC_SEQ), 1)
    for s in range(DEC_SEQ):
        q = (q_ref[0, 0, s:s + 1, :] * scale).astype(BF16)
        kt = jnp.concatenate([k_refs[s * N_SEL_PAGES + r][...] for r in range(N_SEL_PAGES)], axis=1).astype(BF16)
        vt = jnp.concatenate([v_refs[s * N_SEL_PAGES + r][...] for r in range(N_SEL_PAGES)], axis=1).astype(BF16)
        s_sel = jnp.dot(q, kt, preferred_element_type=F32)
        s_own = jnp.where(pos <= s, lax.dot_general(q, k_new, nt, preferred_element_type=F32), NEG)
        m = jnp.maximum(jnp.max(s_sel, axis=-1, keepdims=True), jnp.max(s_own, axis=-1, keepdims=True))
        p_sel = jnp.exp(s_sel - m)
        p_own = jnp.exp(s_own - m)
        l = jnp.sum(p_sel, axis=-1, keepdims=True) + jnp.sum(p_own, axis=-1, keepdims=True)
        o = (lax.dot_general(p_sel.astype(BF16), vt, nt, preferred_element_type=F32)
             + jnp.dot(p_own.astype(BF16), v_new, preferred_element_type=F32))
        o_ref[0, 0, s:s + 1, :] = o / l


def _sample_attn(q_nh, k_nh, v_nh, cache_k, cache_v, page_table, idx, layer):
    def new_spec():
        return pl.BlockSpec((1, 1, DEC_SEQ, HEAD_DIM), lambda n, h, pt, ix: (n, h, 0, 0))

    def page_spec(s, r):
        def index_map(n, h, pt, ix):
            blk = ix[((n * N_HEADS + h) * DEC_SEQ + s) * MOBA_TOP_K + r // PAGES_PER_BLOCK]
            return (layer, pt[n, blk * PAGES_PER_BLOCK + r % PAGES_PER_BLOCK], h, 0, 0)
        return pl.BlockSpec((None, None, None, HEAD_DIM, PAGE_SIZE), index_map)

    gathered = [page_spec(s, r) for s in range(DEC_SEQ) for r in range(N_SEL_PAGES)]
    return pl.pallas_call(
        _sample_attn_kernel,
        grid_spec=pltpu.PrefetchScalarGridSpec(
            num_scalar_prefetch=2,
            grid=(DEC_BATCH, N_HEADS),
            in_specs=[new_spec(), new_spec(), new_spec()] + gathered + gathered,
            out_specs=new_spec(),
        ),
        out_shape=jax.ShapeDtypeStruct((DEC_BATCH, N_HEADS, DEC_SEQ, HEAD_DIM), F32),
        compiler_params=_cparams(2),
        name="sample_attn",
    )(page_table, idx.reshape(-1), q_nh, k_nh, v_nh,
      *([cache_k] * len(gathered)), *([cache_v] * len(gathered)))


def _rope_tables(pos):
    half = HEAD_DIM // 2
    inv = ROPE_THETA ** (-jnp.arange(half, dtype=F32) / half)
    ang = pos.astype(F32)[:, None] * inv[None, :]
    cos, sin = jnp.cos(ang), jnp.sin(ang)
    reps = LANES // half
    cos_t = jnp.tile(cos, (1, reps))
    sin_t = jnp.concatenate([-sin, sin] * (reps // 2), axis=1)
    return cos_t, sin_t


def _to_pages(a2d):
    return a2d.reshape(BATCH, SEQ // PAGE_SIZE, PAGE_SIZE, N_HEADS, HEAD_DIM).transpose(0, 1, 3, 2, 4)


def _sample_heads(a2d):
    return a2d.reshape(DEC_SEQ, DEC_BATCH, N_HEADS, HEAD_DIM).transpose(1, 2, 0, 3)


def _sample_rows(a2d):
    return a2d.reshape(DEC_SEQ, DEC_BATCH, a2d.shape[-1]).transpose(1, 0, 2)


def kernel(x_prompt, x_sample, cache_k, cache_v, state_pool, page_table, w_in, g_v, w_s, b_s, w_pool, pool_scale, w_branch, w_out, g_pre_mix, g_post_mix, g_pre_ffn, g_post_ffn, w_ff_in, w_ff_out):
    cos_p, sin_p = _rope_tables(jnp.arange(SEQ))
    cos_s, sin_s = _rope_tables(PAST_LEN + jnp.arange(SAMPLE_ROWS) // DEC_BATCH)

    xp = x_prompt.reshape(BATCH * SEQ, D_MODEL)
    xs = x_sample.transpose(1, 0, 2).reshape(SAMPLE_ROWS, D_MODEL)
    page_table = page_table.astype(jnp.int32)
    cache_kt = cache_k.transpose(0, 1, 2, 4, 3)
    cache_vt = cache_v.transpose(0, 1, 2, 4, 3)

    kp_l, vp_l, poolp_l, ks_l, vs_l, pools_l, cvs_l = [], [], [], [], [], [], []
    for l in range(DEPTH):
        w_in_bf = w_in[l].astype(BF16)
        w_pool_bf = w_pool[l].astype(BF16)
        w_branch_bf = w_branch[l].astype(BF16)
        w_out_bf = w_out[l].astype(BF16)
        w_ff_in_bf = w_ff_in[l].astype(BF16)
        w_ff_out_bf = w_ff_out[l].astype(BF16)
        row = lambda a: a[l].reshape(1, -1)
        b_s_t = b_s[l].T

        zp = _inproj(xp, row(g_pre_mix), w_in_bf, row(g_v), cos_p, sin_p, tm=1024)
        oab = _prompt_mix(zp, w_s[l], b_s_t, w_pool_bf, row(pool_scale))
        oc = _prompt_attn(zp)
        x1 = _finish(xp, oab, oc, zp, w_branch_bf, w_out_bf, row(g_post_mix), tm=256)
        xp = _ffn(x1, row(g_pre_ffn), w_ff_in_bf, w_ff_out_bf, row(g_post_ffn), tm=512)
        kp_l.append(_to_pages(zp[:, 4 * BRANCH_WIDTH:5 * BRANCH_WIDTH]))
        vp_l.append(_to_pages(zp[:, 5 * BRANCH_WIDTH:6 * BRANCH_WIDTH]))
        poolp_l.append(zp[:, 2 * BRANCH_WIDTH:3 * BRANCH_WIDTH]
                       .reshape(BATCH, SEQ, BRANCH_WIDTH)[:, SEQ - POOL_KEEP:])

        zs = _inproj(xs, row(g_pre_mix), w_in_bf, row(g_v), cos_s, sin_s, tm=SAMPLE_ROWS)
        w_vec = jnp.repeat(w_s[l][:, :DEC_SEQ, :DEC_SEQ].transpose(1, 2, 0), CM_GROUP_WIDTH, axis=-1)
        b_vec = jnp.repeat(b_s[l][:, :DEC_SEQ].T, CM_GROUP_WIDTH, axis=-1)
        state_t = state_pool[l].transpose(1, 0, 2)
        oab_s = _sample_mix(zs, state_t, w_vec, b_vec, w_pool_bf, row(pool_scale))
        q_nh = _sample_heads(zs[:, 3 * BRANCH_WIDTH:4 * BRANCH_WIDTH])
        k_nh = _sample_heads(zs[:, 4 * BRANCH_WIDTH:5 * BRANCH_WIDTH])
        v_nh = _sample_heads(zs[:, 5 * BRANCH_WIDTH:6 * BRANCH_WIDTH])
        km = _kmeans(cache_kt, page_table, l)
        idx = _gate_topk(q_nh, km)
        oc_nh = _sample_attn(q_nh, k_nh, v_nh, cache_kt, cache_vt, page_table, idx, l)
        oc_s = oc_nh.transpose(2, 0, 1, 3).reshape(SAMPLE_ROWS, BRANCH_WIDTH)
        x1s = _finish(xs, oab_s, oc_s, zs, w_branch_bf, w_out_bf, row(g_post_mix), tm=SAMPLE_ROWS)
        xs = _ffn(x1s, row(g_pre_ffn), w_ff_in_bf, w_ff_out_bf, row(g_post_ffn), tm=SAMPLE_ROWS)
        ks_l.append(k_nh)
        vs_l.append(v_nh)
        p_s = _sample_rows(zs[:, 2 * BRANCH_WIDTH:3 * BRANCH_WIDTH])
        pools_l.append(jnp.concatenate([state_pool[l][:, DEC_SEQ:], p_s], axis=1))
        cvs_l.append(_sample_rows(zs[:, BRANCH_WIDTH:2 * BRANCH_WIDTH]))

    y_prompt = xp.reshape(BATCH, SEQ, D_MODEL)
    y_sample = _sample_rows(xs)
    return (y_prompt, y_sample, jnp.stack(kp_l), jnp.stack(vp_l), jnp.stack(poolp_l),
            jnp.stack(ks_l), jnp.stack(vs_l), jnp.stack(pools_l), jnp.stack(cvs_l))
```

```python
import functools

import numpy as np
import jax
import jax.numpy as jnp
from jax import lax
from jax.experimental import pallas as pl
from jax.experimental.pallas import tpu as pltpu

D_MODEL = 1024
BATCH = 4
SEQ = 4096
DEPTH = 2
DEC_BATCH = 32
DEC_SEQ = 4
PAST_LEN = 16384
PAGE_SIZE = 128

BRANCH_WIDTH = D_MODEL // 2
HEAD_DIM = 64
N_HEADS = BRANCH_WIDTH // HEAD_DIM
MOBA_BLOCK = 256
MOBA_TOP_K = 3
ROPE_THETA = 10000.0
CM_CHUNK = 128
CM_GROUPS = 8
CM_GROUP_WIDTH = BRANCH_WIDTH // CM_GROUPS
POOL_WINDOWS = (2, 4, 8, 16)
POOL_GROUP_WIDTH = BRANCH_WIDTH // len(POOL_WINDOWS)
POOL_KEEP = max(POOL_WINDOWS) - 1
N_BRANCH = 3
D_FF = -(-8 * D_MODEL // (3 * 256)) * 256
IN_WIDTH = 6 * BRANCH_WIDTH + N_BRANCH * D_MODEL
NORM_EPS = 1e-6
NEG = -1e30
LOG2_E = 1.4426950408889634

LANES = 128
COL_TILE = BRANCH_WIDTH
N_COL_TILES = IN_WIDTH // COL_TILE
GATE_COL0 = 6 * BRANCH_WIDTH
PAGES_PER_BLOCK = MOBA_BLOCK // PAGE_SIZE
N_PAST_BLOCKS = PAST_LEN // MOBA_BLOCK
N_PROMPT_BLOCKS = SEQ // MOBA_BLOCK
FF_TILE = D_FF // 2
VMEM_LIMIT = 56 * 1024 * 1024

F32 = jnp.float32
BF16 = jnp.bfloat16


def _cparams(n_axes):
    return pltpu.CompilerParams(dimension_semantics=("arbitrary",) * n_axes,
                                vmem_limit_bytes=VMEM_LIMIT)


def _rms(x, g):
    return (x * lax.rsqrt(jnp.mean(x * x, axis=-1, keepdims=True) + NORM_EPS)) * g


def _sigmoid(x):
    return 1.0 / (1.0 + jnp.exp(-x))


def _inproj_kernel(x_ref, g_ref, w_ref, gv_ref, cos_ref, sin_ref, z_ref, *rest, row_chunk, emit_pages):
    xn_ref = rest[-1]
    kt_ref, vt_ref = rest[:2] if emit_pages else (None, None)
    j = pl.program_id(1)
    tm = x_ref.shape[0]
    pages_per_chunk = row_chunk // PAGE_SIZE

    @pl.when(j == 0)
    def _():
        xn_ref[...] = _rms(x_ref[...], g_ref[...]).astype(BF16)

    def tile(epilogue, t_ref=None):
        for r in range(tm // row_chunk):
            rs = slice(r * row_chunk, (r + 1) * row_chunk)
            acc = jnp.dot(xn_ref[rs, :], w_ref[...], preferred_element_type=F32)
            epilogue(acc, rs, r, t_ref)

    def store_pages(t_ref, vals, r, c):
        for pg in range(pages_per_chunk):
            t_ref[r * pages_per_chunk + pg, c * LANES:(c + 1) * LANES, :] = (
                vals[pg * PAGE_SIZE:(pg + 1) * PAGE_SIZE, :].T)

    def gelu_out(acc, rs, r, t_ref):
        z_ref[rs, :] = jax.nn.gelu(acc)

    def gelu_layernorm_out(acc, rs, r, t_ref):
        v = jax.nn.gelu(acc)
        vc = v - jnp.mean(v, axis=-1, keepdims=True)
        y = vc * lax.rsqrt(jnp.mean(vc * vc, axis=-1, keepdims=True) + NORM_EPS)
        z_ref[rs, :] = y * gv_ref[...]

    def raw_out(acc, rs, r, t_ref):
        z_ref[rs, :] = acc
        if t_ref is not None:
            for c in range(COL_TILE // LANES):
                store_pages(t_ref, acc[:, c * LANES:(c + 1) * LANES], r, c)

    def rope_out(acc, rs, r, t_ref):
        lane = lax.broadcasted_iota(jnp.int32, (1, LANES), 1)
        first_half = (lane % HEAD_DIM) < (HEAD_DIM // 2)
        cos = cos_ref[rs, :]
        sin = sin_ref[rs, :]
        for c in range(COL_TILE // LANES):
            a = acc[:, c * LANES:(c + 1) * LANES]
            partner = jnp.where(first_half,
                                pltpu.roll(a, LANES - HEAD_DIM // 2, axis=1),
                                pltpu.roll(a, HEAD_DIM // 2, axis=1))
            rot = a * cos + partner * sin
            z_ref[rs, c * LANES:(c + 1) * LANES] = rot
            if t_ref is not None:
                store_pages(t_ref, rot, r, c)

    def sigmoid_out(acc, rs, r, t_ref):
        z_ref[rs, :] = _sigmoid(acc)

    pl.when(j == 0)(lambda: tile(gelu_out))
    pl.when(j == 1)(lambda: tile(gelu_layernorm_out))
    pl.when(j == 2)(lambda: tile(raw_out))
    pl.when(j == 3)(lambda: tile(rope_out))
    pl.when(j == 4)(lambda: tile(rope_out, kt_ref))
    pl.when(j == 5)(lambda: tile(raw_out, vt_ref))
    pl.when(j >= GATE_COL0 // COL_TILE)(lambda: tile(sigmoid_out))


def _inproj(x2d, g_pre, w_in_bf, g_v, cos_t, sin_t, tm, emit_pages):
    rows = x2d.shape[0]
    t_tiles = cos_t.shape[0] // tm
    z_spec = pl.BlockSpec((tm, COL_TILE), lambda i, j: (i, j))
    z_shape = jax.ShapeDtypeStruct((rows, IN_WIDTH), F32)
    if emit_pages:
        page_spec = pl.BlockSpec((tm // PAGE_SIZE, BRANCH_WIDTH, PAGE_SIZE), lambda i, j: (i, 0, 0))
        page_shape = jax.ShapeDtypeStruct((rows // PAGE_SIZE, BRANCH_WIDTH, PAGE_SIZE), F32)
        out_specs, out_shape = [z_spec, page_spec, page_spec], [z_shape, page_shape, page_shape]
    else:
        out_specs, out_shape = z_spec, z_shape
    return pl.pallas_call(
        functools.partial(_inproj_kernel, row_chunk=min(tm, 256), emit_pages=emit_pages),
        grid=(rows // tm, N_COL_TILES),
        in_specs=[
            pl.BlockSpec((tm, D_MODEL), lambda i, j: (i, 0)),
            pl.BlockSpec((1, D_MODEL), lambda i, j: (0, 0)),
            pl.BlockSpec((D_MODEL, COL_TILE), lambda i, j: (0, j)),
            pl.BlockSpec((1, COL_TILE), lambda i, j: (0, 0)),
            pl.BlockSpec((tm, LANES), lambda i, j: (i % t_tiles, 0)),
            pl.BlockSpec((tm, LANES), lambda i, j: (i % t_tiles, 0)),
        ],
        out_specs=out_specs,
        out_shape=out_shape,
        scratch_shapes=[pltpu.VMEM((tm, D_MODEL), BF16)],
        compiler_params=_cparams(2),
        name="inproj",
    )(x2d, g_pre, w_in_bf, g_v, cos_t, sin_t)


MIX_TILE = 2 * CM_CHUNK
HALO = 16


def _prompt_mix_kernel(u_ref, vn_ref, p_ref, prev_ref, ws_ref, bs_ref, wp_ref, sc_ref,
                       oab_ref, ext_ref):
    i = pl.program_id(0)
    tiles_per_seq = SEQ // MIX_TILE
    first = (i % tiles_per_seq) == 0

    row = lax.broadcasted_iota(jnp.int32, (CM_CHUNK, CM_CHUNK), 0)
    col = lax.broadcasted_iota(jnp.int32, (CM_CHUNK, CM_CHUNK), 1)
    tri = row >= col
    lane = lax.broadcasted_iota(jnp.int32, (1, LANES), 1)
    lo_half = lane < CM_GROUP_WIDTH
    w_tri = [jnp.where(tri, ws_ref[g], 0.0).astype(BF16) for g in range(CM_GROUPS)]
    for c in range(MIX_TILE // CM_CHUNK):
        rs = slice(c * CM_CHUNK, (c + 1) * CM_CHUNK)
        for pr in range(CM_GROUPS // 2):
            cs = slice(pr * LANES, (pr + 1) * LANES)
            vn = vn_ref[rs, cs]
            v_lo = jnp.where(lo_half, vn, 0.0).astype(BF16)
            v_hi = jnp.where(lo_half, 0.0, vn).astype(BF16)
            s = (jnp.dot(w_tri[2 * pr], v_lo, preferred_element_type=F32)
                 + jnp.dot(w_tri[2 * pr + 1], v_hi, preferred_element_type=F32))
            bias = jnp.where(lo_half, bs_ref[:, 2 * pr:2 * pr + 1], bs_ref[:, 2 * pr + 1:2 * pr + 2])
            oab_ref[rs, cs] = u_ref[rs, cs] * (s + bias)

    ext_ref[0:HALO, :] = jnp.where(first, 0.0, prev_ref[...])
    ext_ref[HALO:HALO + MIX_TILE, :] = p_ref[...]
    t_in_seq = (i % tiles_per_seq) * MIX_TILE + lax.broadcasted_iota(jnp.int32, (MIX_TILE, 1), 0)
    for gi, win in enumerate(POOL_WINDOWS):
        cs = slice(gi * POOL_GROUP_WIDTH, (gi + 1) * POOL_GROUP_WIDTH)
        tot = ext_ref[HALO:HALO + MIX_TILE, cs]
        for k in range(1, win):
            tot = tot + ext_ref[HALO - k:HALO - k + MIX_TILE, cs]
        cnt = jnp.minimum(t_in_seq + 1, win).astype(F32)
        d = tot / cnt - p_ref[:, cs]
        y = jnp.dot(d.astype(BF16), wp_ref[gi], preferred_element_type=F32)
        oab_ref[:, BRANCH_WIDTH + gi * POOL_GROUP_WIDTH:BRANCH_WIDTH + (gi + 1) * POOL_GROUP_WIDTH] = y * sc_ref[:, cs]


def _prompt_mix(z, w_s, b_s_t, w_pool_bf, pool_scale):
    rows = z.shape[0]
    halo_blocks = MIX_TILE // HALO
    return pl.pallas_call(
        _prompt_mix_kernel,
        grid=(rows // MIX_TILE,),
        in_specs=[
            pl.BlockSpec((MIX_TILE, BRANCH_WIDTH), lambda i: (i, 0)),
            pl.BlockSpec((MIX_TILE, BRANCH_WIDTH), lambda i: (i, 1)),
            pl.BlockSpec((MIX_TILE, BRANCH_WIDTH), lambda i: (i, 2)),
            pl.BlockSpec((HALO, BRANCH_WIDTH), lambda i: (jnp.maximum(i * halo_blocks - 1, 0), 2)),
            pl.BlockSpec((CM_GROUPS, CM_CHUNK, CM_CHUNK), lambda i: (0, 0, 0)),
            pl.BlockSpec((CM_CHUNK, CM_GROUPS), lambda i: (0, 0)),
            pl.BlockSpec((len(POOL_WINDOWS), POOL_GROUP_WIDTH, POOL_GROUP_WIDTH), lambda i: (0, 0, 0)),
            pl.BlockSpec((1, BRANCH_WIDTH), lambda i: (0, 0)),
        ],
        out_specs=pl.BlockSpec((MIX_TILE, 2 * BRANCH_WIDTH), lambda i: (i, 0)),
        out_shape=jax.ShapeDtypeStruct((rows, 2 * BRANCH_WIDTH), F32),
        scratch_shapes=[pltpu.VMEM((HALO + MIX_TILE, BRANCH_WIDTH), F32)],
        compiler_params=_cparams(1),
        name="prompt_mix",
    )(z, z, z, z, w_s, b_s_t, w_pool_bf, pool_scale)


HEADS_PER_TILE = LANES // HEAD_DIM


def _top_k_rows(gate_t, blk_f, n_valid):
    n_blk = gate_t.shape[0]
    g = jnp.where(blk_f < n_valid, gate_t, NEG)
    sel = jnp.zeros(gate_t.shape, F32)
    for r in range(MOBA_TOP_K):
        mx = jnp.max(g, axis=0, keepdims=True)
        first = jnp.min(jnp.where(g == mx, blk_f, float(n_blk)), axis=0, keepdims=True)
        pick = jnp.where(blk_f == first, jnp.where(n_valid > float(r), 1.0, 0.0), 0.0)
        sel = jnp.maximum(sel, pick)
        g = jnp.where(pick > 0.0, NEG, g)
    return sel


ATTN_GROUP = 4


def _prompt_attn_kernel(q_ref, k_ref, v_ref, o_ref, kbf_ref, vt_ref, km_ref, sel_ref, s_ref, acc_ref):
    b = pl.program_id(2)
    nt = (((1,), (1,)), ((), ()))
    lane = lax.broadcasted_iota(jnp.int32, (1, LANES), 1)

    @pl.when(b == 0)
    def _():
        drow = lax.broadcasted_iota(jnp.int32, (LANES, MOBA_BLOCK), 0)
        for j in range(N_PROMPT_BLOCKS):
            kj = k_ref[j * MOBA_BLOCK:(j + 1) * MOBA_BLOCK, :]
            kbf_ref[j] = kj.astype(BF16)
            km_ref[j:j + 1, :] = jnp.mean(kj, axis=0, keepdims=True)
            vt = v_ref[j * MOBA_BLOCK:(j + 1) * MOBA_BLOCK, :].T
            for hh in range(HEADS_PER_TILE):
                vt_ref[hh, j] = jnp.where((drow // HEAD_DIM) == hh, vt, 1.0).astype(BF16)
        blk_f = lax.broadcasted_iota(jnp.int32, (N_PROMPT_BLOCKS, SEQ), 0).astype(F32)
        own_blk = (lax.broadcasted_iota(jnp.int32, (1, SEQ), 1) // MOBA_BLOCK).astype(F32)
        q_all = q_ref[...]
        for hh in range(HEADS_PER_TILE):
            qh = jnp.where((lane // HEAD_DIM) == hh, q_all, 0.0)
            gate_t = lax.dot_general(km_ref[...], qh, nt, precision=lax.Precision.HIGHEST,
                                     preferred_element_type=F32)
            sel = _top_k_rows(gate_t, blk_f, own_blk)
            for qb in range(N_PROMPT_BLOCKS):
                sel_ref[hh, qb] = sel[:, qb * MOBA_BLOCK:(qb + 1) * MOBA_BLOCK]

    q = q_ref[pl.ds(pl.multiple_of(b * MOBA_BLOCK, MOBA_BLOCK), MOBA_BLOCK), :] * (HEAD_DIM ** -0.5 * LOG2_E)
    q_bf = [jnp.where((lane // HEAD_DIM) == hh, q, 0.0).astype(BF16) for hh in range(HEADS_PER_TILE)]
    key_i = lax.broadcasted_iota(jnp.int32, (MOBA_BLOCK, MOBA_BLOCK), 0)
    qry_i = lax.broadcasted_iota(jnp.int32, (MOBA_BLOCK, MOBA_BLOCK), 1)
    n_chunks = lax.shift_right_logical(b + ATTN_GROUP, ATTN_GROUP.bit_length() - 1)

    def score_chunk(c, slot, ms, first):
        ms = list(ms)
        for t in range(ATTN_GROUP):
            own = first and t == 0
            j = b if own else c * ATTN_GROUP + (t - 1)
            kj = kbf_ref[j]
            for hh in range(HEADS_PER_TILE):
                s = lax.dot_general(kj, q_bf[hh], nt, preferred_element_type=F32)
                if own:
                    s = jnp.where(key_i <= qry_i, s, NEG)
                else:
                    s = jnp.where(sel_ref[hh, b, pl.ds(j, 1), :] > 0.0, s, NEG)
                s_ref[hh, slot, t] = s
                ms[hh] = jnp.maximum(ms[hh], jnp.max(s, axis=0, keepdims=True))
        return tuple(ms)

    def value_chunk(c, slot, m_acc, m_new):
        for hh in range(HEADS_PER_TILE):
            acc = jnp.exp2(m_acc[hh] - m_new[hh]) * acc_ref[hh]
            for t in range(ATTN_GROUP):
                j = c * ATTN_GROUP + (t - 1)
                if t == 0:
                    j = jnp.where(c == 0, b, j)
                p = jnp.exp2(s_ref[hh, slot, t] - m_new[hh]).astype(BF16)
                acc = acc + jnp.dot(vt_ref[hh, j], p, preferred_element_type=F32)
            acc_ref[hh] = acc

    acc_ref[...] = jnp.zeros_like(acc_ref)
    neg_row = jnp.full((1, MOBA_BLOCK), NEG, F32)
    m_first = score_chunk(0, 0, (neg_row,) * HEADS_PER_TILE, True)

    def pipelined(c, carry):
        m_acc, m_prev = carry
        value_chunk(c - 1, (c - 1) & 1, m_acc, m_prev)
        m_cur = score_chunk(c, c & 1, m_prev, False)
        return m_prev, m_cur

    m_acc, m_last = lax.fori_loop(1, n_chunks, pipelined, (m_first, m_first))
    value_chunk(n_chunks - 1, (n_chunks - 1) & 1, m_acc, m_last)

    drow = lax.broadcasted_iota(jnp.int32, (LANES, MOBA_BLOCK), 0)
    a0 = acc_ref[0]
    a1 = acc_ref[1]
    out_t = jnp.where(drow < HEAD_DIM, a0 / a0[HEAD_DIM:HEAD_DIM + 1, :], a1 / a1[0:1, :])
    o_ref[...] = out_t.T


def _prompt_attn(z):
    rows = z.shape[0]
    q0 = 3 * BRANCH_WIDTH // LANES
    k0 = 4 * BRANCH_WIDTH // LANES
    v0 = 5 * BRANCH_WIDTH // LANES
    assert HEADS_PER_TILE == 2 and N_PROMPT_BLOCKS % ATTN_GROUP == 0
    return pl.pallas_call(
        _prompt_attn_kernel,
        grid=(BATCH, BRANCH_WIDTH // LANES, N_PROMPT_BLOCKS),
        in_specs=[
            pl.BlockSpec((SEQ, LANES), lambda n, hp, b: (n, q0 + hp)),
            pl.BlockSpec((SEQ, LANES), lambda n, hp, b: (n, k0 + hp)),
            pl.BlockSpec((SEQ, LANES), lambda n, hp, b: (n, v0 + hp)),
        ],
        out_specs=pl.BlockSpec((MOBA_BLOCK, LANES), lambda n, hp, b: (n * N_PROMPT_BLOCKS + b, hp)),
        out_shape=jax.ShapeDtypeStruct((rows, BRANCH_WIDTH), F32),
        scratch_shapes=[pltpu.VMEM((N_PROMPT_BLOCKS, MOBA_BLOCK, LANES), BF16),
                        pltpu.VMEM((HEADS_PER_TILE, N_PROMPT_BLOCKS, LANES, MOBA_BLOCK), BF16),
                        pltpu.VMEM((N_PROMPT_BLOCKS, LANES), F32),
                        pltpu.VMEM((HEADS_PER_TILE, N_PROMPT_BLOCKS, N_PROMPT_BLOCKS, MOBA_BLOCK), F32),
                        pltpu.VMEM((HEADS_PER_TILE, 2, ATTN_GROUP, MOBA_BLOCK, MOBA_BLOCK), F32),
                        pltpu.VMEM((HEADS_PER_TILE, LANES, MOBA_BLOCK), F32)],
        compiler_params=_cparams(3),
        name="prompt_attn",
    )(z, z, z)


def _finish_kernel(x_ref, oab_ref, oc_ref, gates_ref, wb_ref, wo_ref, g_ref, o_ref):
    br_a = jnp.dot(oab_ref[:, 0:BRANCH_WIDTH].astype(BF16), wb_ref[0], preferred_element_type=F32)
    br_b = jnp.dot(oab_ref[:, BRANCH_WIDTH:2 * BRANCH_WIDTH].astype(BF16), wb_ref[1], preferred_element_type=F32)
    br_c = jnp.dot(oc_ref[...].astype(BF16), wb_ref[2], preferred_element_type=F32)
    mixed = (gates_ref[:, 0:D_MODEL] * br_a + gates_ref[:, D_MODEL:2 * D_MODEL] * br_b
             + gates_ref[:, 2 * D_MODEL:3 * D_MODEL] * br_c)
    mix = jnp.dot(mixed.astype(BF16), wo_ref[...], preferred_element_type=F32)
    o_ref[...] = x_ref[...] + _rms(mix, g_ref[...])


def _finish(x2d, oab, oc, z, w_branch_bf, w_out_bf, g_post_mix, tm):
    rows = x2d.shape[0]
    return pl.pallas_call(
        _finish_kernel,
        grid=(rows // tm,),
        in_specs=[
            pl.BlockSpec((tm, D_MODEL), lambda i: (i, 0)),
            pl.BlockSpec((tm, 2 * BRANCH_WIDTH), lambda i: (i, 0)),
            pl.BlockSpec((tm, BRANCH_WIDTH), lambda i: (i, 0)),
            pl.BlockSpec((tm, N_BRANCH * D_MODEL), lambda i: (i, 1)),
            pl.BlockSpec((N_BRANCH, BRANCH_WIDTH, D_MODEL), lambda i: (0, 0, 0)),
            pl.BlockSpec((D_MODEL, D_MODEL), lambda i: (0, 0)),
            pl.BlockSpec((1, D_MODEL), lambda i: (0, 0)),
        ],
        out_specs=pl.BlockSpec((tm, D_MODEL), lambda i: (i, 0)),
        out_shape=jax.ShapeDtypeStruct((rows, D_MODEL), F32),
        compiler_params=_cparams(1),
        name="finish",
    )(x2d, oab, oc, z, w_branch_bf, w_out_bf, g_post_mix)


def _ffn_kernel(x_ref, gpre_ref, wg_ref, wu_ref, wo_ref, gpost_ref, o_ref, xn_ref, acc_ref):
    f = pl.program_id(1)

    @pl.when(f == 0)
    def _():
        xn_ref[...] = _rms(x_ref[...], gpre_ref[...]).astype(BF16)
        acc_ref[...] = jnp.zeros_like(acc_ref)

    xn = xn_ref[...]
    gt = jnp.dot(xn, wg_ref[...], preferred_element_type=F32)
    up = jnp.dot(xn, wu_ref[...], preferred_element_type=F32)
    h = (gt * _sigmoid(gt)) * up
    acc_ref[...] += jnp.dot(h.astype(BF16), wo_ref[...], preferred_element_type=F32)

    @pl.when(f == pl.num_programs(1) - 1)
    def _():
        o_ref[...] = x_ref[...] + _rms(acc_ref[...], gpost_ref[...])


def _ffn(x2d, g_pre, w_ff_in_bf, w_ff_out_bf, g_post, tm):
    rows = x2d.shape[0]
    n_f = D_FF // FF_TILE
    return pl.pallas_call(
        _ffn_kernel,
        grid=(rows // tm, n_f),
        in_specs=[
            pl.BlockSpec((tm, D_MODEL), lambda i, f: (i, 0)),
            pl.BlockSpec((1, D_MODEL), lambda i, f: (0, 0)),
            pl.BlockSpec((D_MODEL, FF_TILE), lambda i, f: (0, f)),
            pl.BlockSpec((D_MODEL, FF_TILE), lambda i, f: (0, n_f + f)),
            pl.BlockSpec((FF_TILE, D_MODEL), lambda i, f: (f, 0)),
            pl.BlockSpec((1, D_MODEL), lambda i, f: (0, 0)),
        ],
        out_specs=pl.BlockSpec((tm, D_MODEL), lambda i, f: (i, 0)),
        out_shape=jax.ShapeDtypeStruct((rows, D_MODEL), F32),
        scratch_shapes=[pltpu.VMEM((tm, D_MODEL), BF16), pltpu.VMEM((tm, D_MODEL), F32)],
        compiler_params=_cparams(2),
        name="ffn",
    )(x2d, g_pre, w_ff_in_bf, w_ff_in_bf, w_ff_out_bf, g_post)


SAMPLE_ROWS = DEC_BATCH * DEC_SEQ


def _sample_mix_kernel(u_ref, vn_ref, p_ref, st_ref, wv_ref, bv_ref, wp_ref, sc_ref, oab_ref):
    def slab(ref, s):
        return ref[s * DEC_BATCH:(s + 1) * DEC_BATCH, :]

    for t in range(DEC_SEQ):
        s_t = bv_ref[t:t + 1, :]
        for sp in range(t + 1):
            s_t = s_t + wv_ref[t, sp:sp + 1, :] * slab(vn_ref, sp)
        oab_ref[t * DEC_BATCH:(t + 1) * DEC_BATCH, 0:BRANCH_WIDTH] = slab(u_ref, t) * s_t

    ext = [st_ref[r] for r in range(POOL_KEEP)] + [slab(p_ref, s) for s in range(DEC_SEQ)]
    lane = lax.broadcasted_iota(jnp.int32, (1, BRANCH_WIDTH), 1)
    d_rows = []
    for s in range(DEC_SEQ):
        end = POOL_KEEP + s + 1
        mean = jnp.zeros((DEC_BATCH, BRANCH_WIDTH), F32)
        run = jnp.zeros((DEC_BATCH, BRANCH_WIDTH), F32)
        taken = 0
        for gi, win in enumerate(POOL_WINDOWS):
            for r in range(end - win, end - taken):
                run = run + ext[r]
            taken = win
            mean = jnp.where((lane // POOL_GROUP_WIDTH) == gi, run / float(win), mean)
        d_rows.append(mean - ext[POOL_KEEP + s])
    d = jnp.concatenate(d_rows, axis=0).astype(BF16)
    for gi in range(len(POOL_WINDOWS)):
        cs = slice(gi * POOL_GROUP_WIDTH, (gi + 1) * POOL_GROUP_WIDTH)
        y = jnp.dot(d[:, cs], wp_ref[gi], preferred_element_type=F32)
        oab_ref[:, BRANCH_WIDTH + gi * POOL_GROUP_WIDTH:BRANCH_WIDTH + (gi + 1) * POOL_GROUP_WIDTH] = y * sc_ref[:, cs]


def _sample_mix(z, state_t, w_vec, b_vec, w_pool_bf, pool_scale):
    return pl.pallas_call(
        _sample_mix_kernel,
        grid=(1,),
        in_specs=[
            pl.BlockSpec((SAMPLE_ROWS, BRANCH_WIDTH), lambda i: (0, 0)),
            pl.BlockSpec((SAMPLE_ROWS, BRANCH_WIDTH), lambda i: (0, 1)),
            pl.BlockSpec((SAMPLE_ROWS, BRANCH_WIDTH), lambda i: (0, 2)),
            pl.BlockSpec((POOL_KEEP, DEC_BATCH, BRANCH_WIDTH), lambda i: (0, 0, 0)),
            pl.BlockSpec((DEC_SEQ, DEC_SEQ, BRANCH_WIDTH), lambda i: (0, 0, 0)),
            pl.BlockSpec((DEC_SEQ, BRANCH_WIDTH), lambda i: (0, 0)),
            pl.BlockSpec((len(POOL_WINDOWS), POOL_GROUP_WIDTH, POOL_GROUP_WIDTH), lambda i: (0, 0, 0)),
            pl.BlockSpec((1, BRANCH_WIDTH), lambda i: (0, 0)),
        ],
        out_specs=pl.BlockSpec((SAMPLE_ROWS, 2 * BRANCH_WIDTH), lambda i: (0, 0)),
        out_shape=jax.ShapeDtypeStruct((SAMPLE_ROWS, 2 * BRANCH_WIDTH), F32),
        compiler_params=_cparams(1),
        name="sample_mix",
    )(z, z, z, state_t, w_vec, b_vec, w_pool_bf, pool_scale)


KM_PAGES = 16


def _kmeans_kernel(pt_ref, *refs):
    del pt_ref
    page_refs, km_ref = refs[:KM_PAGES], refs[KM_PAGES]
    c = pl.program_id(1)

    @pl.when(c == 0)
    def _():
        km_ref[...] = jnp.zeros_like(km_ref)

    blocks_per_step = KM_PAGES // PAGES_PER_BLOCK
    blk = lax.broadcasted_iota(jnp.int32, (1, 1, N_PAST_BLOCKS), 2)
    acc = km_ref[0]
    for jj in range(blocks_per_step):
        tot = page_refs[PAGES_PER_BLOCK * jj][...]
        for r in range(1, PAGES_PER_BLOCK):
            tot = tot + page_refs[PAGES_PER_BLOCK * jj + r][...]
        mean = jnp.sum(tot, axis=-1, keepdims=True) * (1.0 / MOBA_BLOCK)
        acc = jnp.where(blk == c * blocks_per_step + jj, mean, acc)
    km_ref[0] = acc


def _kmeans(cache_kt, page_table, layer):
    n_pages = PAST_LEN // PAGE_SIZE

    def page_spec(r):
        return pl.BlockSpec((None, None, N_HEADS, HEAD_DIM, PAGE_SIZE),
                            lambda n, c, pt: (layer, pt[n, c * KM_PAGES + r], 0, 0, 0))

    return pl.pallas_call(
        _kmeans_kernel,
        grid_spec=pltpu.PrefetchScalarGridSpec(
            num_scalar_prefetch=1,
            grid=(DEC_BATCH, n_pages // KM_PAGES),
            in_specs=[page_spec(r) for r in range(KM_PAGES)],
            out_specs=pl.BlockSpec((1, N_HEADS, HEAD_DIM, N_PAST_BLOCKS), lambda n, c, pt: (n, 0, 0, 0)),
        ),
        out_shape=jax.ShapeDtypeStruct((DEC_BATCH, N_HEADS, HEAD_DIM, N_PAST_BLOCKS), F32),
        compiler_params=_cparams(2),
        name="kmeans",
    )(page_table, *([cache_kt] * KM_PAGES))


def _gate_topk_kernel(q_ref, km_ref, idx_ref):
    lane_f = lax.broadcasted_iota(jnp.int32, (DEC_SEQ, N_PAST_BLOCKS), 1).astype(F32)
    for h in range(N_HEADS):
        gate = jnp.dot(q_ref[0, h], km_ref[0, h], precision=lax.Precision.HIGHEST,
                       preferred_element_type=F32)
        g = gate
        for r in range(MOBA_TOP_K):
            mx = jnp.max(g, axis=-1, keepdims=True)
            first = jnp.min(jnp.where(g == mx, lane_f, float(N_PAST_BLOCKS)), axis=-1, keepdims=True)
            idx_ref[0, h, :, r:r + 1] = first.astype(jnp.int32)
            g = jnp.where(lane_f == first, NEG, g)


def _gate_topk(q_nh, km_nh):
    return pl.pallas_call(
        _gate_topk_kernel,
        grid=(DEC_BATCH,),
        in_specs=[
            pl.BlockSpec((1, N_HEADS, DEC_SEQ, HEAD_DIM), lambda n: (n, 0, 0, 0)),
            pl.BlockSpec((1, N_HEADS, HEAD_DIM, N_PAST_BLOCKS), lambda n: (n, 0, 0, 0)),
        ],
        out_specs=pl.BlockSpec((1, N_HEADS, DEC_SEQ, MOBA_TOP_K), lambda n: (n, 0, 0, 0)),
        out_shape=jax.ShapeDtypeStruct((DEC_BATCH, N_HEADS, DEC_SEQ, MOBA_TOP_K), jnp.int32),
        compiler_params=_cparams(1),
        name="gate_topk",
    )(q_nh, km_nh)


N_SEL_PAGES = MOBA_TOP_K * PAGES_PER_BLOCK


def _sample_attn_kernel(pt_ref, idx_ref, q_ref, kn_ref, vn_ref, *refs):
    del pt_ref, idx_ref
    n_g = DEC_SEQ * N_SEL_PAGES
    k_refs, v_refs, o_ref = refs[:n_g], refs[n_g:2 * n_g], refs[2 * n_g]
    scale = HEAD_DIM ** -0.5
    k_new = kn_ref[0, 0]
    v_new = vn_ref[0, 0]
    pos = lax.broadcasted_iota(jnp.int32, (1, DEC_SEQ), 1)
    for s in range(DEC_SEQ):
        q = q_ref[0, 0, :, s:s + 1] * scale
        s_sel = [jnp.sum(q * k_refs[s * N_SEL_PAGES + r][...], axis=0, keepdims=True)
                 for r in range(N_SEL_PAGES)]
        s_own = jnp.where(pos <= s, jnp.sum(q * k_new, axis=0, keepdims=True), NEG)
        m_sel = functools.reduce(jnp.maximum, s_sel)
        m = jnp.maximum(jnp.max(m_sel, axis=-1, keepdims=True), jnp.max(s_own, axis=-1, keepdims=True))
        p_sel = [jnp.exp(sr - m) for sr in s_sel]
        p_own = jnp.exp(s_own - m)
        l = (jnp.sum(functools.reduce(jnp.add, p_sel), axis=-1, keepdims=True)
             + jnp.sum(p_own, axis=-1, keepdims=True))
        pv = functools.reduce(jnp.add, [v_refs[s * N_SEL_PAGES + r][...] * p_sel[r]
                                        for r in range(N_SEL_PAGES)])
        o = jnp.sum(pv, axis=-1, keepdims=True) + jnp.sum(v_new * p_own, axis=-1, keepdims=True)
        o_ref[0, 0, :, s:s + 1] = o / l


def _sample_attn(q_nh, k_nh, v_nh, cache_k, cache_v, page_table, idx, layer):
    def new_spec():
        return pl.BlockSpec((1, 1, HEAD_DIM, DEC_SEQ), lambda n, h, pt, ix: (n, h, 0, 0))

    def page_spec(s, r):
        def index_map(n, h, pt, ix):
            blk = ix[((n * N_HEADS + h) * DEC_SEQ + s) * MOBA_TOP_K + r // PAGES_PER_BLOCK]
            return (layer, pt[n, blk * PAGES_PER_BLOCK + r % PAGES_PER_BLOCK], h, 0, 0)
        return pl.BlockSpec((None, None, None, HEAD_DIM, PAGE_SIZE), index_map)

    gathered = [page_spec(s, r) for s in range(DEC_SEQ) for r in range(N_SEL_PAGES)]
    return pl.pallas_call(
        _sample_attn_kernel,
        grid_spec=pltpu.PrefetchScalarGridSpec(
            num_scalar_prefetch=2,
            grid=(DEC_BATCH, N_HEADS),
            in_specs=[new_spec(), new_spec(), new_spec()] + gathered + gathered,
            out_specs=new_spec(),
        ),
        out_shape=jax.ShapeDtypeStruct((DEC_BATCH, N_HEADS, HEAD_DIM, DEC_SEQ), F32),
        compiler_params=_cparams(2),
        name="sample_attn",
    )(page_table, idx.reshape(-1), q_nh, k_nh, v_nh,
      *([cache_k] * len(gathered)), *([cache_v] * len(gathered)))


def _rope_tables(pos):
    half = HEAD_DIM // 2
    inv = ROPE_THETA ** (-jnp.arange(half, dtype=F32) / half)
    ang = pos.astype(F32)[:, None] * inv[None, :]
    cos, sin = jnp.cos(ang), jnp.sin(ang)
    reps = LANES // half
    cos_t = jnp.tile(cos, (1, reps))
    sin_t = jnp.concatenate([-sin, sin] * (reps // 2), axis=1)
    return cos_t, sin_t


def _pages_out(per_layer):
    t = jnp.stack(per_layer).reshape(DEPTH, BATCH, SEQ // PAGE_SIZE, N_HEADS, HEAD_DIM, PAGE_SIZE)
    return t.transpose(0, 1, 2, 3, 5, 4)


def _sample_heads(a2d):
    return a2d.reshape(DEC_SEQ, DEC_BATCH, N_HEADS, HEAD_DIM).transpose(1, 2, 0, 3)


def _sample_rows(a2d):
    return a2d.reshape(DEC_SEQ, DEC_BATCH, a2d.shape[-1]).transpose(1, 0, 2)


def kernel(x_prompt, x_sample, cache_k, cache_v, state_pool, page_table, w_in, g_v, w_s, b_s, w_pool, pool_scale, w_branch, w_out, g_pre_mix, g_post_mix, g_pre_ffn, g_post_ffn, w_ff_in, w_ff_out):
    cos_p, sin_p = _rope_tables(jnp.arange(SEQ))
    cos_s, sin_s = _rope_tables(PAST_LEN + jnp.arange(SAMPLE_ROWS) // DEC_BATCH)

    xp = x_prompt.reshape(BATCH * SEQ, D_MODEL)
    xs = x_sample.transpose(1, 0, 2).reshape(SAMPLE_ROWS, D_MODEL)
    page_table = page_table.astype(jnp.int32)
    cache_kt = cache_k.transpose(0, 1, 2, 4, 3)
    cache_vt = cache_v.transpose(0, 1, 2, 4, 3)

    kp_l, vp_l, poolp_l, ks_l, vs_l, pools_l, cvs_l = [], [], [], [], [], [], []
    for l in range(DEPTH):
        w_in_bf = w_in[l].astype(BF16)
        w_pool_bf = w_pool[l].astype(BF16)
        w_branch_bf = w_branch[l].astype(BF16)
        w_out_bf = w_out[l].astype(BF16)
        w_ff_in_bf = w_ff_in[l].astype(BF16)
        w_ff_out_bf = w_ff_out[l].astype(BF16)
        row = lambda a: a[l].reshape(1, -1)
        b_s_t = b_s[l].T

        zp, kt_pages, vt_pages = _inproj(xp, row(g_pre_mix), w_in_bf, row(g_v), cos_p, sin_p,
                                         tm=1024, emit_pages=True)
        oab = _prompt_mix(zp, w_s[l], b_s_t, w_pool_bf, row(pool_scale))
        oc = _prompt_attn(zp)
        x1 = _finish(xp, oab, oc, zp, w_branch_bf, w_out_bf, row(g_post_mix), tm=256)
        xp = _ffn(x1, row(g_pre_ffn), w_ff_in_bf, w_ff_out_bf, row(g_post_ffn), tm=512)
        kp_l.append(kt_pages)
        vp_l.append(vt_pages)
        poolp_l.append(zp[:, 2 * BRANCH_WIDTH:3 * BRANCH_WIDTH]
                       .reshape(BATCH, SEQ, BRANCH_WIDTH)[:, SEQ - POOL_KEEP:])

        zs = _inproj(xs, row(g_pre_mix), w_in_bf, row(g_v), cos_s, sin_s, tm=SAMPLE_ROWS,
                     emit_pages=False)
        w_vec = jnp.repeat(w_s[l][:, :DEC_SEQ, :DEC_SEQ].transpose(1, 2, 0), CM_GROUP_WIDTH, axis=-1)
        b_vec = jnp.repeat(b_s[l][:, :DEC_SEQ].T, CM_GROUP_WIDTH, axis=-1)
        state_t = state_pool[l].transpose(1, 0, 2)
        oab_s = _sample_mix(zs, state_t, w_vec, b_vec, w_pool_bf, row(pool_scale))
        q_nh = _sample_heads(zs[:, 3 * BRANCH_WIDTH:4 * BRANCH_WIDTH])
        k_nh = _sample_heads(zs[:, 4 * BRANCH_WIDTH:5 * BRANCH_WIDTH])
        v_nh = _sample_heads(zs[:, 5 * BRANCH_WIDTH:6 * BRANCH_WIDTH])
        km = _kmeans(cache_kt, page_table, l)
        idx = _gate_topk(q_nh, km)
        to_cols = lambda a: a.transpose(0, 1, 3, 2)
        oc_cols = _sample_attn(to_cols(q_nh), to_cols(k_nh), to_cols(v_nh), cache_kt, cache_vt,
                               page_table, idx, l)
        oc_s = oc_cols.transpose(3, 0, 1, 2).reshape(SAMPLE_ROWS, BRANCH_WIDTH)
        x1s = _finish(xs, oab_s, oc_s, zs, w_branch_bf, w_out_bf, row(g_post_mix), tm=SAMPLE_ROWS)
        xs = _ffn(x1s, row(g_pre_ffn), w_ff_in_bf, w_ff_out_bf, row(g_post_ffn), tm=SAMPLE_ROWS)
        ks_l.append(k_nh)
        vs_l.append(v_nh)
        p_s = _sample_rows(zs[:, 2 * BRANCH_WIDTH:3 * BRANCH_WIDTH])
        pools_l.append(jnp.concatenate([state_pool[l][:, DEC_SEQ:], p_s], axis=1))
        cvs_l.append(_sample_rows(zs[:, BRANCH_WIDTH:2 * BRANCH_WIDTH]))

    y_prompt = xp.reshape(BATCH, SEQ, D_MODEL)
    y_sample = _sample_rows(xs)
    return (y_prompt, y_sample, _pages_out(kp_l), _pages_out(vp_l), jnp.stack(poolp_l),
            jnp.stack(ks_l), jnp.stack(vs_l), jnp.stack(pools_l), jnp.stack(cvs_l))
```

```python
import functools

import numpy as np
import jax
import jax.numpy as jnp
from jax import lax
from jax.experimental import pallas as pl
from jax.experimental.pallas import tpu as pltpu

D_MODEL = 1024
BATCH = 4
SEQ = 4096
DEPTH = 2
DEC_BATCH = 32
DEC_SEQ = 4
PAST_LEN = 16384
PAGE_SIZE = 128

BRANCH_WIDTH = D_MODEL // 2
HEAD_DIM = 64
N_HEADS = BRANCH_WIDTH // HEAD_DIM
MOBA_BLOCK = 256
MOBA_TOP_K = 3
ROPE_THETA = 10000.0
CM_CHUNK = 128
CM_GROUPS = 8
CM_GROUP_WIDTH = BRANCH_WIDTH // CM_GROUPS
POOL_WINDOWS = (2, 4, 8, 16)
POOL_GROUP_WIDTH = BRANCH_WIDTH // len(POOL_WINDOWS)
POOL_KEEP = max(POOL_WINDOWS) - 1
N_BRANCH = 3
D_FF = -(-8 * D_MODEL // (3 * 256)) * 256
IN_WIDTH = 6 * BRANCH_WIDTH + N_BRANCH * D_MODEL
NORM_EPS = 1e-6
NEG = -1e30
LOG2_E = 1.4426950408889634

LANES = 128
COL_TILE = BRANCH_WIDTH
N_COL_TILES = IN_WIDTH // COL_TILE
GATE_COL0 = 6 * BRANCH_WIDTH
PAGES_PER_BLOCK = MOBA_BLOCK // PAGE_SIZE
N_PAST_BLOCKS = PAST_LEN // MOBA_BLOCK
N_PROMPT_BLOCKS = SEQ // MOBA_BLOCK
FF_TILE = D_FF // 2
VMEM_LIMIT = 56 * 1024 * 1024

F32 = jnp.float32
BF16 = jnp.bfloat16


def _cparams(n_axes):
    return pltpu.CompilerParams(dimension_semantics=("arbitrary",) * n_axes,
                                vmem_limit_bytes=VMEM_LIMIT)


def _rms(x, g):
    return (x * lax.rsqrt(jnp.mean(x * x, axis=-1, keepdims=True) + NORM_EPS)) * g


def _sigmoid(x):
    return 1.0 / (1.0 + jnp.exp(-x))


def _inproj_kernel(x_ref, g_ref, w_ref, gv_ref, cos_ref, sin_ref, z_ref, *rest, row_chunk, emit_pages):
    xn_ref = rest[-1]
    kt_ref, vt_ref = rest[:2] if emit_pages else (None, None)
    j = pl.program_id(1)
    tm = x_ref.shape[0]
    pages_per_chunk = row_chunk // PAGE_SIZE

    @pl.when(j == 0)
    def _():
        xn_ref[...] = _rms(x_ref[...], g_ref[...]).astype(BF16)

    def tile(epilogue, t_ref=None):
        for r in range(tm // row_chunk):
            rs = slice(r * row_chunk, (r + 1) * row_chunk)
            acc = jnp.dot(xn_ref[rs, :], w_ref[...], preferred_element_type=F32)
            epilogue(acc, rs, r, t_ref)

    def store_pages(t_ref, vals, r, c):
        for pg in range(pages_per_chunk):
            t_ref[r * pages_per_chunk + pg, c * LANES:(c + 1) * LANES, :] = (
                vals[pg * PAGE_SIZE:(pg + 1) * PAGE_SIZE, :].T)

    def gelu_out(acc, rs, r, t_ref):
        z_ref[rs, :] = jax.nn.gelu(acc)

    def gelu_layernorm_out(acc, rs, r, t_ref):
        v = jax.nn.gelu(acc)
        vc = v - jnp.mean(v, axis=-1, keepdims=True)
        y = vc * lax.rsqrt(jnp.mean(vc * vc, axis=-1, keepdims=True) + NORM_EPS)
        z_ref[rs, :] = y * gv_ref[...]

    def raw_out(acc, rs, r, t_ref):
        z_ref[rs, :] = acc
        if t_ref is not None:
            for c in range(COL_TILE // LANES):
                store_pages(t_ref, acc[:, c * LANES:(c + 1) * LANES], r, c)

    def rope_out(acc, rs, r, t_ref):
        lane = lax.broadcasted_iota(jnp.int32, (1, LANES), 1)
        first_half = (lane % HEAD_DIM) < (HEAD_DIM // 2)
        cos = cos_ref[rs, :]
        sin = sin_ref[rs, :]
        for c in range(COL_TILE // LANES):
            a = acc[:, c * LANES:(c + 1) * LANES]
            partner = jnp.where(first_half,
                                pltpu.roll(a, LANES - HEAD_DIM // 2, axis=1),
                                pltpu.roll(a, HEAD_DIM // 2, axis=1))
            rot = a * cos + partner * sin
            z_ref[rs, c * LANES:(c + 1) * LANES] = rot
            if t_ref is not None:
                store_pages(t_ref, rot, r, c)

    def sigmoid_out(acc, rs, r, t_ref):
        z_ref[rs, :] = _sigmoid(acc)

    pl.when(j == 0)(lambda: tile(gelu_out))
    pl.when(j == 1)(lambda: tile(gelu_layernorm_out))
    pl.when(j == 2)(lambda: tile(raw_out))
    pl.when(j == 3)(lambda: tile(rope_out))
    pl.when(j == 4)(lambda: tile(rope_out, kt_ref))
    pl.when(j == 5)(lambda: tile(raw_out, vt_ref))
    pl.when(j >= GATE_COL0 // COL_TILE)(lambda: tile(sigmoid_out))


def _inproj(x2d, g_pre, w_in_bf, g_v, cos_t, sin_t, tm, emit_pages):
    rows = x2d.shape[0]
    t_tiles = cos_t.shape[0] // tm
    z_spec = pl.BlockSpec((tm, COL_TILE), lambda i, j: (i, j))
    z_shape = jax.ShapeDtypeStruct((rows, IN_WIDTH), F32)
    if emit_pages:
        page_spec = pl.BlockSpec((tm // PAGE_SIZE, BRANCH_WIDTH, PAGE_SIZE), lambda i, j: (i, 0, 0))
        page_shape = jax.ShapeDtypeStruct((rows // PAGE_SIZE, BRANCH_WIDTH, PAGE_SIZE), F32)
        out_specs, out_shape = [z_spec, page_spec, page_spec], [z_shape, page_shape, page_shape]
    else:
        out_specs, out_shape = z_spec, z_shape
    return pl.pallas_call(
        functools.partial(_inproj_kernel, row_chunk=min(tm, 256), emit_pages=emit_pages),
        grid=(rows // tm, N_COL_TILES),
        in_specs=[
            pl.BlockSpec((tm, D_MODEL), lambda i, j: (i, 0)),
            pl.BlockSpec((1, D_MODEL), lambda i, j: (0, 0)),
            pl.BlockSpec((D_MODEL, COL_TILE), lambda i, j: (0, j)),
            pl.BlockSpec((1, COL_TILE), lambda i, j: (0, 0)),
            pl.BlockSpec((tm, LANES), lambda i, j: (i % t_tiles, 0)),
            pl.BlockSpec((tm, LANES), lambda i, j: (i % t_tiles, 0)),
        ],
        out_specs=out_specs,
        out_shape=out_shape,
        scratch_shapes=[pltpu.VMEM((tm, D_MODEL), BF16)],
        compiler_params=_cparams(2),
        name="inproj",
    )(x2d, g_pre, w_in_bf, g_v, cos_t, sin_t)


MIX_TILE = 2 * CM_CHUNK
HALO = 16


def _prompt_mix_kernel(u_ref, vn_ref, p_ref, prev_ref, ws_ref, bs_ref, wp_ref, sc_ref,
                       oab_ref, ext_ref):
    i = pl.program_id(0)
    tiles_per_seq = SEQ // MIX_TILE
    first = (i % tiles_per_seq) == 0

    row = lax.broadcasted_iota(jnp.int32, (CM_CHUNK, CM_CHUNK), 0)
    col = lax.broadcasted_iota(jnp.int32, (CM_CHUNK, CM_CHUNK), 1)
    tri = row >= col
    lane = lax.broadcasted_iota(jnp.int32, (1, LANES), 1)
    lo_half = lane < CM_GROUP_WIDTH
    w_tri = [jnp.where(tri, ws_ref[g], 0.0).astype(BF16) for g in range(CM_GROUPS)]
    for c in range(MIX_TILE // CM_CHUNK):
        rs = slice(c * CM_CHUNK, (c + 1) * CM_CHUNK)
        for pr in range(CM_GROUPS // 2):
            cs = slice(pr * LANES, (pr + 1) * LANES)
            vn = vn_ref[rs, cs]
            v_lo = jnp.where(lo_half, vn, 0.0).astype(BF16)
            v_hi = jnp.where(lo_half, 0.0, vn).astype(BF16)
            s = (jnp.dot(w_tri[2 * pr], v_lo, preferred_element_type=F32)
                 + jnp.dot(w_tri[2 * pr + 1], v_hi, preferred_element_type=F32))
            bias = jnp.where(lo_half, bs_ref[:, 2 * pr:2 * pr + 1], bs_ref[:, 2 * pr + 1:2 * pr + 2])
            oab_ref[rs, cs] = u_ref[rs, cs] * (s + bias)

    ext_ref[0:HALO, :] = jnp.where(first, 0.0, prev_ref[...])
    ext_ref[HALO:HALO + MIX_TILE, :] = p_ref[...]
    t_in_seq = (i % tiles_per_seq) * MIX_TILE + lax.broadcasted_iota(jnp.int32, (MIX_TILE, 1), 0)
    for gi, win in enumerate(POOL_WINDOWS):
        cs = slice(gi * POOL_GROUP_WIDTH, (gi + 1) * POOL_GROUP_WIDTH)
        tot = ext_ref[HALO:HALO + MIX_TILE, cs]
        for k in range(1, win):
            tot = tot + ext_ref[HALO - k:HALO - k + MIX_TILE, cs]
        cnt = jnp.minimum(t_in_seq + 1, win).astype(F32)
        d = tot / cnt - p_ref[:, cs]
        y = jnp.dot(d.astype(BF16), wp_ref[gi], preferred_element_type=F32)
        oab_ref[:, BRANCH_WIDTH + gi * POOL_GROUP_WIDTH:BRANCH_WIDTH + (gi + 1) * POOL_GROUP_WIDTH] = y * sc_ref[:, cs]


def _prompt_mix(z, w_s, b_s_t, w_pool_bf, pool_scale):
    rows = z.shape[0]
    halo_blocks = MIX_TILE // HALO
    return pl.pallas_call(
        _prompt_mix_kernel,
        grid=(rows // MIX_TILE,),
        in_specs=[
            pl.BlockSpec((MIX_TILE, BRANCH_WIDTH), lambda i: (i, 0)),
            pl.BlockSpec((MIX_TILE, BRANCH_WIDTH), lambda i: (i, 1)),
            pl.BlockSpec((MIX_TILE, BRANCH_WIDTH), lambda i: (i, 2)),
            pl.BlockSpec((HALO, BRANCH_WIDTH), lambda i: (jnp.maximum(i * halo_blocks - 1, 0), 2)),
            pl.BlockSpec((CM_GROUPS, CM_CHUNK, CM_CHUNK), lambda i: (0, 0, 0)),
            pl.BlockSpec((CM_CHUNK, CM_GROUPS), lambda i: (0, 0)),
            pl.BlockSpec((len(POOL_WINDOWS), POOL_GROUP_WIDTH, POOL_GROUP_WIDTH), lambda i: (0, 0, 0)),
            pl.BlockSpec((1, BRANCH_WIDTH), lambda i: (0, 0)),
        ],
        out_specs=pl.BlockSpec((MIX_TILE, 2 * BRANCH_WIDTH), lambda i: (i, 0)),
        out_shape=jax.ShapeDtypeStruct((rows, 2 * BRANCH_WIDTH), F32),
        scratch_shapes=[pltpu.VMEM((HALO + MIX_TILE, BRANCH_WIDTH), F32)],
        compiler_params=_cparams(1),
        name="prompt_mix",
    )(z, z, z, z, w_s, b_s_t, w_pool_bf, pool_scale)


HEADS_PER_TILE = LANES // HEAD_DIM


def _top_k_rows(gate_t, blk_f, n_valid):
    n_blk = gate_t.shape[0]
    g = jnp.where(blk_f < n_valid, gate_t, NEG)
    sel = jnp.zeros(gate_t.shape, F32)
    for r in range(MOBA_TOP_K):
        mx = jnp.max(g, axis=0, keepdims=True)
        first = jnp.min(jnp.where(g == mx, blk_f, float(n_blk)), axis=0, keepdims=True)
        pick = jnp.where(blk_f == first, jnp.where(n_valid > float(r), 1.0, 0.0), 0.0)
        sel = jnp.maximum(sel, pick)
        g = jnp.where(pick > 0.0, NEG, g)
    return sel


ATTN_GROUP = 4


def _prompt_attn_kernel(q_ref, k_ref, v_ref, o_ref, kbf_ref, vt_ref, km_ref, sel_ref, s_ref, acc_ref):
    b = pl.program_id(2)
    nt = (((1,), (1,)), ((), ()))
    lane = lax.broadcasted_iota(jnp.int32, (1, LANES), 1)

    @pl.when(b == 0)
    def _():
        drow = lax.broadcasted_iota(jnp.int32, (LANES, MOBA_BLOCK), 0)
        for j in range(N_PROMPT_BLOCKS):
            kj = k_ref[j * MOBA_BLOCK:(j + 1) * MOBA_BLOCK, :]
            kbf_ref[j] = kj.astype(BF16)
            km_ref[j:j + 1, :] = jnp.mean(kj, axis=0, keepdims=True)
            vt = v_ref[j * MOBA_BLOCK:(j + 1) * MOBA_BLOCK, :].T
            for hh in range(HEADS_PER_TILE):
                vt_ref[hh, j] = jnp.where((drow // HEAD_DIM) == hh, vt, 1.0).astype(BF16)
        blk_f = lax.broadcasted_iota(jnp.int32, (N_PROMPT_BLOCKS, SEQ), 0).astype(F32)
        own_blk = (lax.broadcasted_iota(jnp.int32, (1, SEQ), 1) // MOBA_BLOCK).astype(F32)
        q_all = q_ref[...]
        for hh in range(HEADS_PER_TILE):
            qh = jnp.where((lane // HEAD_DIM) == hh, q_all, 0.0)
            gate_t = lax.dot_general(km_ref[...], qh, nt, precision=lax.Precision.HIGHEST,
                                     preferred_element_type=F32)
            sel = _top_k_rows(gate_t, blk_f, own_blk)
            for qb in range(N_PROMPT_BLOCKS):
                sel_ref[hh, qb] = sel[:, qb * MOBA_BLOCK:(qb + 1) * MOBA_BLOCK]

    q = q_ref[pl.ds(pl.multiple_of(b * MOBA_BLOCK, MOBA_BLOCK), MOBA_BLOCK), :] * (HEAD_DIM ** -0.5 * LOG2_E)
    q_bf = [jnp.where((lane // HEAD_DIM) == hh, q, 0.0).astype(BF16) for hh in range(HEADS_PER_TILE)]
    key_i = lax.broadcasted_iota(jnp.int32, (MOBA_BLOCK, MOBA_BLOCK), 0)
    qry_i = lax.broadcasted_iota(jnp.int32, (MOBA_BLOCK, MOBA_BLOCK), 1)
    n_chunks = lax.shift_right_logical(b + ATTN_GROUP, ATTN_GROUP.bit_length() - 1)

    def score_chunk(c, slot, ms, first):
        ms = list(ms)
        for t in range(ATTN_GROUP):
            own = first and t == 0
            j = b if own else c * ATTN_GROUP + (t - 1)
            kj = kbf_ref[j]
            for hh in range(HEADS_PER_TILE):
                s = lax.dot_general(kj, q_bf[hh], nt, preferred_element_type=F32)
                if own:
                    s = jnp.where(key_i <= qry_i, s, NEG)
                else:
                    s = jnp.where(sel_ref[hh, b, pl.ds(j, 1), :] > 0.0, s, NEG)
                s_ref[hh, slot, t] = s
                ms[hh] = jnp.maximum(ms[hh], jnp.max(s, axis=0, keepdims=True))
        return tuple(ms)

    def value_chunk(c, slot, m_acc, m_new):
        for hh in range(HEADS_PER_TILE):
            acc = jnp.exp2(m_acc[hh] - m_new[hh]) * acc_ref[hh]
            for t in range(ATTN_GROUP):
                j = c * ATTN_GROUP + (t - 1)
                if t == 0:
                    j = jnp.where(c == 0, b, j)
                p = jnp.exp2(s_ref[hh, slot, t] - m_new[hh]).astype(BF16)
                acc = acc + jnp.dot(vt_ref[hh, j], p, preferred_element_type=F32)
            acc_ref[hh] = acc

    acc_ref[...] = jnp.zeros_like(acc_ref)
    neg_row = jnp.full((1, MOBA_BLOCK), NEG, F32)
    m_first = score_chunk(0, 0, (neg_row,) * HEADS_PER_TILE, True)

    def pipelined(c, carry):
        m_acc, m_prev = carry
        value_chunk(c - 1, (c - 1) & 1, m_acc, m_prev)
        m_cur = score_chunk(c, c & 1, m_prev, False)
        return m_prev, m_cur

    m_acc, m_last = lax.fori_loop(1, n_chunks, pipelined, (m_first, m_first))
    value_chunk(n_chunks - 1, (n_chunks - 1) & 1, m_acc, m_last)

    drow = lax.broadcasted_iota(jnp.int32, (LANES, MOBA_BLOCK), 0)
    a0 = acc_ref[0]
    a1 = acc_ref[1]
    out_t = jnp.where(drow < HEAD_DIM, a0 / a0[HEAD_DIM:HEAD_DIM + 1, :], a1 / a1[0:1, :])
    o_ref[...] = out_t.T


def _prompt_attn(z):
    rows = z.shape[0]
    q0 = 3 * BRANCH_WIDTH // LANES
    k0 = 4 * BRANCH_WIDTH // LANES
    v0 = 5 * BRANCH_WIDTH // LANES
    assert HEADS_PER_TILE == 2 and N_PROMPT_BLOCKS % ATTN_GROUP == 0
    return pl.pallas_call(
        _prompt_attn_kernel,
        grid=(BATCH, BRANCH_WIDTH // LANES, N_PROMPT_BLOCKS),
        in_specs=[
            pl.BlockSpec((SEQ, LANES), lambda n, hp, b: (n, q0 + hp)),
            pl.BlockSpec((SEQ, LANES), lambda n, hp, b: (n, k0 + hp)),
            pl.BlockSpec((SEQ, LANES), lambda n, hp, b: (n, v0 + hp)),
        ],
        out_specs=pl.BlockSpec((MOBA_BLOCK, LANES), lambda n, hp, b: (n * N_PROMPT_BLOCKS + b, hp)),
        out_shape=jax.ShapeDtypeStruct((rows, BRANCH_WIDTH), F32),
        scratch_shapes=[pltpu.VMEM((N_PROMPT_BLOCKS, MOBA_BLOCK, LANES), BF16),
                        pltpu.VMEM((HEADS_PER_TILE, N_PROMPT_BLOCKS, LANES, MOBA_BLOCK), BF16),
                        pltpu.VMEM((N_PROMPT_BLOCKS, LANES), F32),
                        pltpu.VMEM((HEADS_PER_TILE, N_PROMPT_BLOCKS, N_PROMPT_BLOCKS, MOBA_BLOCK), F32),
                        pltpu.VMEM((HEADS_PER_TILE, 2, ATTN_GROUP, MOBA_BLOCK, MOBA_BLOCK), F32),
                        pltpu.VMEM((HEADS_PER_TILE, LANES, MOBA_BLOCK), F32)],
        compiler_params=_cparams(3),
        name="prompt_attn",
    )(z, z, z)


def _finish_kernel(x_ref, oab_ref, oc_ref, gates_ref, wb_ref, wo_ref, g_ref, o_ref):
    br_a = jnp.dot(oab_ref[:, 0:BRANCH_WIDTH].astype(BF16), wb_ref[0], preferred_element_type=F32)
    br_b = jnp.dot(oab_ref[:, BRANCH_WIDTH:2 * BRANCH_WIDTH].astype(BF16), wb_ref[1], preferred_element_type=F32)
    br_c = jnp.dot(oc_ref[...].astype(BF16), wb_ref[2], preferred_element_type=F32)
    mixed = (gates_ref[:, 0:D_MODEL] * br_a + gates_ref[:, D_MODEL:2 * D_MODEL] * br_b
             + gates_ref[:, 2 * D_MODEL:3 * D_MODEL] * br_c)
    mix = jnp.dot(mixed.astype(BF16), wo_ref[...], preferred_element_type=F32)
    o_ref[...] = x_ref[...] + _rms(mix, g_ref[...])


def _finish(x2d, oab, oc, z, w_branch_bf, w_out_bf, g_post_mix, tm):
    rows = x2d.shape[0]
    return pl.pallas_call(
        _finish_kernel,
        grid=(rows // tm,),
        in_specs=[
            pl.BlockSpec((tm, D_MODEL), lambda i: (i, 0)),
            pl.BlockSpec((tm, 2 * BRANCH_WIDTH), lambda i: (i, 0)),
            pl.BlockSpec((tm, BRANCH_WIDTH), lambda i: (i, 0)),
            pl.BlockSpec((tm, N_BRANCH * D_MODEL), lambda i: (i, 1)),
            pl.BlockSpec((N_BRANCH, BRANCH_WIDTH, D_MODEL), lambda i: (0, 0, 0)),
            pl.BlockSpec((D_MODEL, D_MODEL), lambda i: (0, 0)),
            pl.BlockSpec((1, D_MODEL), lambda i: (0, 0)),
        ],
        out_specs=pl.BlockSpec((tm, D_MODEL), lambda i: (i, 0)),
        out_shape=jax.ShapeDtypeStruct((rows, D_MODEL), F32),
        compiler_params=_cparams(1),
        name="finish",
    )(x2d, oab, oc, z, w_branch_bf, w_out_bf, g_post_mix)


def _ffn_kernel(x_ref, gpre_ref, wg_ref, wu_ref, wo_ref, gpost_ref, o_ref, xn_ref, acc_ref):
    f = pl.program_id(1)

    @pl.when(f == 0)
    def _():
        xn_ref[...] = _rms(x_ref[...], gpre_ref[...]).astype(BF16)
        acc_ref[...] = jnp.zeros_like(acc_ref)

    xn = xn_ref[...]
    gt = jnp.dot(xn, wg_ref[...], preferred_element_type=F32)
    up = jnp.dot(xn, wu_ref[...], preferred_element_type=F32)
    h = (gt * _sigmoid(gt)) * up
    acc_ref[...] += jnp.dot(h.astype(BF16), wo_ref[...], preferred_element_type=F32)

    @pl.when(f == pl.num_programs(1) - 1)
    def _():
        o_ref[...] = x_ref[...] + _rms(acc_ref[...], gpost_ref[...])


def _ffn(x2d, g_pre, w_ff_in_bf, w_ff_out_bf, g_post, tm):
    rows = x2d.shape[0]
    n_f = D_FF // FF_TILE
    return pl.pallas_call(
        _ffn_kernel,
        grid=(rows // tm, n_f),
        in_specs=[
            pl.BlockSpec((tm, D_MODEL), lambda i, f: (i, 0)),
            pl.BlockSpec((1, D_MODEL), lambda i, f: (0, 0)),
            pl.BlockSpec((D_MODEL, FF_TILE), lambda i, f: (0, f)),
            pl.BlockSpec((D_MODEL, FF_TILE), lambda i, f: (0, n_f + f)),
            pl.BlockSpec((FF_TILE, D_MODEL), lambda i, f: (f, 0)),
            pl.BlockSpec((1, D_MODEL), lambda i, f: (0, 0)),
        ],
        out_specs=pl.BlockSpec((tm, D_MODEL), lambda i, f: (i, 0)),
        out_shape=jax.ShapeDtypeStruct((rows, D_MODEL), F32),
        scratch_shapes=[pltpu.VMEM((tm, D_MODEL), BF16), pltpu.VMEM((tm, D_MODEL), F32)],
        compiler_params=_cparams(2),
        name="ffn",
    )(x2d, g_pre, w_ff_in_bf, w_ff_in_bf, w_ff_out_bf, g_post)


SAMPLE_ROWS = DEC_BATCH * DEC_SEQ


def _sample_mix_kernel(u_ref, vn_ref, p_ref, st_ref, wv_ref, bv_ref, wp_ref, sc_ref, oab_ref):
    def slab(ref, s):
        return ref[s * DEC_BATCH:(s + 1) * DEC_BATCH, :]

    for t in range(DEC_SEQ):
        s_t = bv_ref[t:t + 1, :]
        for sp in range(t + 1):
            s_t = s_t + wv_ref[t, sp:sp + 1, :] * slab(vn_ref, sp)
        oab_ref[t * DEC_BATCH:(t + 1) * DEC_BATCH, 0:BRANCH_WIDTH] = slab(u_ref, t) * s_t

    ext = [st_ref[r] for r in range(POOL_KEEP)] + [slab(p_ref, s) for s in range(DEC_SEQ)]
    lane = lax.broadcasted_iota(jnp.int32, (1, BRANCH_WIDTH), 1)
    d_rows = []
    for s in range(DEC_SEQ):
        end = POOL_KEEP + s + 1
        mean = jnp.zeros((DEC_BATCH, BRANCH_WIDTH), F32)
        run = jnp.zeros((DEC_BATCH, BRANCH_WIDTH), F32)
        taken = 0
        for gi, win in enumerate(POOL_WINDOWS):
            for r in range(end - win, end - taken):
                run = run + ext[r]
            taken = win
            mean = jnp.where((lane // POOL_GROUP_WIDTH) == gi, run / float(win), mean)
        d_rows.append(mean - ext[POOL_KEEP + s])
    d = jnp.concatenate(d_rows, axis=0).astype(BF16)
    for gi in range(len(POOL_WINDOWS)):
        cs = slice(gi * POOL_GROUP_WIDTH, (gi + 1) * POOL_GROUP_WIDTH)
        y = jnp.dot(d[:, cs], wp_ref[gi], preferred_element_type=F32)
        oab_ref[:, BRANCH_WIDTH + gi * POOL_GROUP_WIDTH:BRANCH_WIDTH + (gi + 1) * POOL_GROUP_WIDTH] = y * sc_ref[:, cs]


def _sample_mix(z, state_t, w_vec, b_vec, w_pool_bf, pool_scale):
    return pl.pallas_call(
        _sample_mix_kernel,
        grid=(1,),
        in_specs=[
            pl.BlockSpec((SAMPLE_ROWS, BRANCH_WIDTH), lambda i: (0, 0)),
            pl.BlockSpec((SAMPLE_ROWS, BRANCH_WIDTH), lambda i: (0, 1)),
            pl.BlockSpec((SAMPLE_ROWS, BRANCH_WIDTH), lambda i: (0, 2)),
            pl.BlockSpec((POOL_KEEP, DEC_BATCH, BRANCH_WIDTH), lambda i: (0, 0, 0)),
            pl.BlockSpec((DEC_SEQ, DEC_SEQ, BRANCH_WIDTH), lambda i: (0, 0, 0)),
            pl.BlockSpec((DEC_SEQ, BRANCH_WIDTH), lambda i: (0, 0)),
            pl.BlockSpec((len(POOL_WINDOWS), POOL_GROUP_WIDTH, POOL_GROUP_WIDTH), lambda i: (0, 0, 0)),
            pl.BlockSpec((1, BRANCH_WIDTH), lambda i: (0, 0)),
        ],
        out_specs=pl.BlockSpec((SAMPLE_ROWS, 2 * BRANCH_WIDTH), lambda i: (0, 0)),
        out_shape=jax.ShapeDtypeStruct((SAMPLE_ROWS, 2 * BRANCH_WIDTH), F32),
        compiler_params=_cparams(1),
        name="sample_mix",
    )(z, z, z, state_t, w_vec, b_vec, w_pool_bf, pool_scale)


KM_PAGES = 16


def _kmeans_kernel(pt_ref, *refs):
    del pt_ref
    page_refs, km_ref = refs[:KM_PAGES], refs[KM_PAGES]
    c = pl.program_id(1)

    @pl.when(c == 0)
    def _():
        km_ref[...] = jnp.zeros_like(km_ref)

    blocks_per_step = KM_PAGES // PAGES_PER_BLOCK
    blk = lax.broadcasted_iota(jnp.int32, (1, 1, N_PAST_BLOCKS), 2)
    acc = km_ref[0]
    for jj in range(blocks_per_step):
        tot = page_refs[PAGES_PER_BLOCK * jj][...]
        for r in range(1, PAGES_PER_BLOCK):
            tot = tot + page_refs[PAGES_PER_BLOCK * jj + r][...]
        mean = jnp.sum(tot, axis=-1, keepdims=True) * (1.0 / MOBA_BLOCK)
        acc = jnp.where(blk == c * blocks_per_step + jj, mean, acc)
    km_ref[0] = acc


def _kmeans(cache_kt, page_table, layer):
    n_pages = PAST_LEN // PAGE_SIZE

    def page_spec(r):
        return pl.BlockSpec((None, None, N_HEADS, HEAD_DIM, PAGE_SIZE),
                            lambda n, c, pt: (layer, pt[n, c * KM_PAGES + r], 0, 0, 0))

    return pl.pallas_call(
        _kmeans_kernel,
        grid_spec=pltpu.PrefetchScalarGridSpec(
            num_scalar_prefetch=1,
            grid=(DEC_BATCH, n_pages // KM_PAGES),
            in_specs=[page_spec(r) for r in range(KM_PAGES)],
            out_specs=pl.BlockSpec((1, N_HEADS, HEAD_DIM, N_PAST_BLOCKS), lambda n, c, pt: (n, 0, 0, 0)),
        ),
        out_shape=jax.ShapeDtypeStruct((DEC_BATCH, N_HEADS, HEAD_DIM, N_PAST_BLOCKS), F32),
        compiler_params=_cparams(2),
        name="kmeans",
    )(page_table, *([cache_kt] * KM_PAGES))


def _gate_topk_kernel(q_ref, km_ref, idx_ref):
    lane_f = lax.broadcasted_iota(jnp.int32, (DEC_SEQ, N_PAST_BLOCKS), 1).astype(F32)
    for h in range(N_HEADS):
        gate = jnp.dot(q_ref[0, h], km_ref[0, h], precision=lax.Precision.HIGHEST,
                       preferred_element_type=F32)
        g = gate
        for r in range(MOBA_TOP_K):
            mx = jnp.max(g, axis=-1, keepdims=True)
            first = jnp.min(jnp.where(g == mx, lane_f, float(N_PAST_BLOCKS)), axis=-1, keepdims=True)
            idx_ref[0, h, :, r:r + 1] = first.astype(jnp.int32)
            g = jnp.where(lane_f == first, NEG, g)


def _gate_topk(q_nh, km_nh):
    return pl.pallas_call(
        _gate_topk_kernel,
        grid=(DEC_BATCH,),
        in_specs=[
            pl.BlockSpec((1, N_HEADS, DEC_SEQ, HEAD_DIM), lambda n: (n, 0, 0, 0)),
            pl.BlockSpec((1, N_HEADS, HEAD_DIM, N_PAST_BLOCKS), lambda n: (n, 0, 0, 0)),
        ],
        out_specs=pl.BlockSpec((1, N_HEADS, DEC_SEQ, MOBA_TOP_K), lambda n: (n, 0, 0, 0)),
        out_shape=jax.ShapeDtypeStruct((DEC_BATCH, N_HEADS, DEC_SEQ, MOBA_TOP_K), jnp.int32),
        compiler_params=_cparams(1),
        name="gate_topk",
    )(q_nh, km_nh)


N_SEL_PAGES = MOBA_TOP_K * PAGES_PER_BLOCK


N_SLABS = DEC_SEQ * N_SEL_PAGES


def _sample_attn_kernel(pt_ref, idx_ref, q_ref, kn_ref, vn_ref, ck_ref, cv_ref, o_ref,
                        kbuf, vbuf, sem, *, layer):
    n = pl.program_id(0)
    slot = n & 1

    def slab_copies(seq, buf_slot, h, i):
        s, r = divmod(i, N_SEL_PAGES)
        blk = idx_ref[((seq * N_HEADS + h) * DEC_SEQ + s) * MOBA_TOP_K + r // PAGES_PER_BLOCK]
        page = pt_ref[seq, blk * PAGES_PER_BLOCK + r % PAGES_PER_BLOCK]
        return (pltpu.make_async_copy(ck_ref.at[layer, page, h], kbuf.at[buf_slot, h, i], sem.at[buf_slot]),
                pltpu.make_async_copy(cv_ref.at[layer, page, h], vbuf.at[buf_slot, h, i], sem.at[buf_slot]))

    def for_each_slab(seq, buf_slot, fn):
        def per_head(h, carry):
            for i in range(N_SLABS):
                for cp in slab_copies(seq, buf_slot, h, i):
                    fn(cp)
            return carry
        lax.fori_loop(0, N_HEADS, per_head, 0)

    @pl.when(n == 0)
    def _():
        for_each_slab(n, slot, lambda cp: cp.start())

    @pl.when(n + 1 < pl.num_programs(0))
    def _():
        for_each_slab(n + 1, 1 - slot, lambda cp: cp.start())

    for_each_slab(n, slot, lambda cp: cp.wait())

    scale = HEAD_DIM ** -0.5
    pos = lax.broadcasted_iota(jnp.int32, (1, DEC_SEQ), 1)

    def per_head(h, carry):
        k_new = kn_ref[0, h]
        v_new = vn_ref[0, h]
        for s in range(DEC_SEQ):
            q = q_ref[0, h, :, s:s + 1] * scale
            s_sel = [jnp.sum(q * kbuf[slot, h, s * N_SEL_PAGES + r], axis=0, keepdims=True)
                     for r in range(N_SEL_PAGES)]
            s_own = jnp.where(pos <= s, jnp.sum(q * k_new, axis=0, keepdims=True), NEG)
            m_sel = functools.reduce(jnp.maximum, s_sel)
            m = jnp.maximum(jnp.max(m_sel, axis=-1, keepdims=True), jnp.max(s_own, axis=-1, keepdims=True))
            p_sel = [jnp.exp(sr - m) for sr in s_sel]
            p_own = jnp.exp(s_own - m)
            l = (jnp.sum(functools.reduce(jnp.add, p_sel), axis=-1, keepdims=True)
                 + jnp.sum(p_own, axis=-1, keepdims=True))
            pv = functools.reduce(jnp.add, [vbuf[slot, h, s * N_SEL_PAGES + r] * p_sel[r]
                                            for r in range(N_SEL_PAGES)])
            o = jnp.sum(pv, axis=-1, keepdims=True) + jnp.sum(v_new * p_own, axis=-1, keepdims=True)
            o_ref[0, h, :, s:s + 1] = o / l
        return carry

    lax.fori_loop(0, N_HEADS, per_head, 0)


def _sample_attn(q_cols, k_cols, v_cols, cache_kt, cache_vt, page_table, idx, layer):
    def new_spec():
        return pl.BlockSpec((1, N_HEADS, HEAD_DIM, DEC_SEQ), lambda n, pt, ix: (n, 0, 0, 0))

    buf = pltpu.VMEM((2, N_HEADS, N_SLABS, HEAD_DIM, PAGE_SIZE), F32)
    return pl.pallas_call(
        functools.partial(_sample_attn_kernel, layer=layer),
        grid_spec=pltpu.PrefetchScalarGridSpec(
            num_scalar_prefetch=2,
            grid=(DEC_BATCH,),
            in_specs=[new_spec(), new_spec(), new_spec(),
                      pl.BlockSpec(memory_space=pl.ANY), pl.BlockSpec(memory_space=pl.ANY)],
            out_specs=new_spec(),
            scratch_shapes=[buf, buf, pltpu.SemaphoreType.DMA((2,))],
        ),
        out_shape=jax.ShapeDtypeStruct((DEC_BATCH, N_HEADS, HEAD_DIM, DEC_SEQ), F32),
        compiler_params=_cparams(1),
        name="sample_attn",
    )(page_table, idx.reshape(-1), q_cols, k_cols, v_cols, cache_kt, cache_vt)


def _rope_tables(pos):
    half = HEAD_DIM // 2
    inv = ROPE_THETA ** (-jnp.arange(half, dtype=F32) / half)
    ang = pos.astype(F32)[:, None] * inv[None, :]
    cos, sin = jnp.cos(ang), jnp.sin(ang)
    reps = LANES // half
    cos_t = jnp.tile(cos, (1, reps))
    sin_t = jnp.concatenate([-sin, sin] * (reps // 2), axis=1)
    return cos_t, sin_t


def _pages_out(per_layer):
    t = jnp.stack(per_layer).reshape(DEPTH, BATCH, SEQ // PAGE_SIZE, N_HEADS, HEAD_DIM, PAGE_SIZE)
    return t.transpose(0, 1, 2, 3, 5, 4)


def _sample_heads(a2d):
    return a2d.reshape(DEC_SEQ, DEC_BATCH, N_HEADS, HEAD_DIM).transpose(1, 2, 0, 3)


def _sample_rows(a2d):
    return a2d.reshape(DEC_SEQ, DEC_BATCH, a2d.shape[-1]).transpose(1, 0, 2)


def kernel(x_prompt, x_sample, cache_k, cache_v, state_pool, page_table, w_in, g_v, w_s, b_s, w_pool, pool_scale, w_branch, w_out, g_pre_mix, g_post_mix, g_pre_ffn, g_post_ffn, w_ff_in, w_ff_out):
    cos_p, sin_p = _rope_tables(jnp.arange(SEQ))
    cos_s, sin_s = _rope_tables(PAST_LEN + jnp.arange(SAMPLE_ROWS) // DEC_BATCH)

    xp = x_prompt.reshape(BATCH * SEQ, D_MODEL)
    xs = x_sample.transpose(1, 0, 2).reshape(SAMPLE_ROWS, D_MODEL)
    page_table = page_table.astype(jnp.int32)
    cache_kt = cache_k.transpose(0, 1, 2, 4, 3)
    cache_vt = cache_v.transpose(0, 1, 2, 4, 3)

    kp_l, vp_l, poolp_l, ks_l, vs_l, pools_l, cvs_l = [], [], [], [], [], [], []
    for l in range(DEPTH):
        w_in_bf = w_in[l].astype(BF16)
        w_pool_bf = w_pool[l].astype(BF16)
        w_branch_bf = w_branch[l].astype(BF16)
        w_out_bf = w_out[l].astype(BF16)
        w_ff_in_bf = w_ff_in[l].astype(BF16)
        w_ff_out_bf = w_ff_out[l].astype(BF16)
        row = lambda a: a[l].reshape(1, -1)
        b_s_t = b_s[l].T

        zp, kt_pages, vt_pages = _inproj(xp, row(g_pre_mix), w_in_bf, row(g_v), cos_p, sin_p,
                                         tm=1024, emit_pages=True)
        oab = _prompt_mix(zp, w_s[l], b_s_t, w_pool_bf, row(pool_scale))
        oc = _prompt_attn(zp)
        x1 = _finish(xp, oab, oc, zp, w_branch_bf, w_out_bf, row(g_post_mix), tm=256)
        xp = _ffn(x1, row(g_pre_ffn), w_ff_in_bf, w_ff_out_bf, row(g_post_ffn), tm=512)
        kp_l.append(kt_pages)
        vp_l.append(vt_pages)
        poolp_l.append(zp[:, 2 * BRANCH_WIDTH:3 * BRANCH_WIDTH]
                       .reshape(BATCH, SEQ, BRANCH_WIDTH)[:, SEQ - POOL_KEEP:])

        zs = _inproj(xs, row(g_pre_mix), w_in_bf, row(g_v), cos_s, sin_s, tm=SAMPLE_ROWS,
                     emit_pages=False)
        w_vec = jnp.repeat(w_s[l][:, :DEC_SEQ, :DEC_SEQ].transpose(1, 2, 0), CM_GROUP_WIDTH, axis=-1)
        b_vec = jnp.repeat(b_s[l][:, :DEC_SEQ].T, CM_GROUP_WIDTH, axis=-1)
        state_t = state_pool[l].transpose(1, 0, 2)
        oab_s = _sample_mix(zs, state_t, w_vec, b_vec, w_pool_bf, row(pool_scale))
        q_nh = _sample_heads(zs[:, 3 * BRANCH_WIDTH:4 * BRANCH_WIDTH])
        k_nh = _sample_heads(zs[:, 4 * BRANCH_WIDTH:5 * BRANCH_WIDTH])
        v_nh = _sample_heads(zs[:, 5 * BRANCH_WIDTH:6 * BRANCH_WIDTH])
        km = _kmeans(cache_kt, page_table, l)
        idx = _gate_topk(q_nh, km)
        to_cols = lambda a: a.transpose(0, 1, 3, 2)
        oc_cols = _sample_attn(to_cols(q_nh), to_cols(k_nh), to_cols(v_nh), cache_kt, cache_vt,
                               page_table, idx, l)
        oc_s = oc_cols.transpose(3, 0, 1, 2).reshape(SAMPLE_ROWS, BRANCH_WIDTH)
        x1s = _finish(xs, oab_s, oc_s, zs, w_branch_bf, w_out_bf, row(g_post_mix), tm=SAMPLE_ROWS)
        xs = _ffn(x1s, row(g_pre_ffn), w_ff_in_bf, w_ff_out_bf, row(g_post_ffn), tm=SAMPLE_ROWS)
        ks_l.append(k_nh)
        vs_l.append(v_nh)
        p_s = _sample_rows(zs[:, 2 * BRANCH_WIDTH:3 * BRANCH_WIDTH])
        pools_l.append(jnp.concatenate([state_pool[l][:, DEC_SEQ:], p_s], axis=1))
        cvs_l.append(_sample_rows(zs[:, BRANCH_WIDTH:2 * BRANCH_WIDTH]))

    y_prompt = xp.reshape(BATCH, SEQ, D_MODEL)
    y_sample = _sample_rows(xs)
    return (y_prompt, y_sample, _pages_out(kp_l), _pages_out(vp_l), jnp.stack(poolp_l),
            jnp.stack(ks_l), jnp.stack(vs_l), jnp.stack(pools_l), jnp.stack(cvs_l))
```

```python
import functools

import numpy as np
import jax
import jax.numpy as jnp
from jax import lax
from jax.experimental import pallas as pl
from jax.experimental.pallas import tpu as pltpu

D_MODEL = 1024
BATCH = 4
SEQ = 4096
DEPTH = 2
DEC_BATCH = 32
DEC_SEQ = 4
PAST_LEN = 16384
PAGE_SIZE = 128

BRANCH_WIDTH = D_MODEL // 2
HEAD_DIM = 64
N_HEADS = BRANCH_WIDTH // HEAD_DIM
MOBA_BLOCK = 256
MOBA_TOP_K = 3
ROPE_THETA = 10000.0
CM_CHUNK = 128
CM_GROUPS = 8
CM_GROUP_WIDTH = BRANCH_WIDTH // CM_GROUPS
POOL_WINDOWS = (2, 4, 8, 16)
POOL_GROUP_WIDTH = BRANCH_WIDTH // len(POOL_WINDOWS)
POOL_KEEP = max(POOL_WINDOWS) - 1
N_BRANCH = 3
D_FF = -(-8 * D_MODEL // (3 * 256)) * 256
IN_WIDTH = 6 * BRANCH_WIDTH + N_BRANCH * D_MODEL
NORM_EPS = 1e-6
NEG = -1e30
LOG2_E = 1.4426950408889634

LANES = 128
COL_TILE = BRANCH_WIDTH
N_COL_TILES = IN_WIDTH // COL_TILE
GATE_COL0 = 6 * BRANCH_WIDTH
PAGES_PER_BLOCK = MOBA_BLOCK // PAGE_SIZE
N_PAST_BLOCKS = PAST_LEN // MOBA_BLOCK
N_PROMPT_BLOCKS = SEQ // MOBA_BLOCK
FF_TILE = D_FF // 2
VMEM_LIMIT = 56 * 1024 * 1024

F32 = jnp.float32
BF16 = jnp.bfloat16


def _cparams(n_axes):
    return pltpu.CompilerParams(dimension_semantics=("arbitrary",) * n_axes,
                                vmem_limit_bytes=VMEM_LIMIT)


def _rms(x, g):
    return (x * lax.rsqrt(jnp.mean(x * x, axis=-1, keepdims=True) + NORM_EPS)) * g


def _sigmoid(x):
    return 1.0 / (1.0 + jnp.exp(-x))


def _inproj_kernel(x_ref, g_ref, w_ref, gv_ref, cos_ref, sin_ref, z_ref, *rest, row_chunk, emit_pages):
    xn_ref = rest[-1]
    kt_ref, vt_ref = rest[:2] if emit_pages else (None, None)
    j = pl.program_id(1)
    tm = x_ref.shape[0]
    pages_per_chunk = row_chunk // PAGE_SIZE

    @pl.when(j == 0)
    def _():
        xn_ref[...] = _rms(x_ref[...], g_ref[...]).astype(BF16)

    def tile(epilogue, t_ref=None):
        for r in range(tm // row_chunk):
            rs = slice(r * row_chunk, (r + 1) * row_chunk)
            acc = jnp.dot(xn_ref[rs, :], w_ref[...], preferred_element_type=F32)
            epilogue(acc, rs, r, t_ref)

    def store_pages(t_ref, vals, r, c):
        for pg in range(pages_per_chunk):
            t_ref[r * pages_per_chunk + pg, c * LANES:(c + 1) * LANES, :] = (
                vals[pg * PAGE_SIZE:(pg + 1) * PAGE_SIZE, :].T)

    def gelu_out(acc, rs, r, t_ref):
        z_ref[rs, :] = jax.nn.gelu(acc).astype(z_ref.dtype)

    def gelu_layernorm_out(acc, rs, r, t_ref):
        v = jax.nn.gelu(acc)
        vc = v - jnp.mean(v, axis=-1, keepdims=True)
        y = vc * lax.rsqrt(jnp.mean(vc * vc, axis=-1, keepdims=True) + NORM_EPS)
        z_ref[rs, :] = (y * gv_ref[...]).astype(z_ref.dtype)

    def raw_out(acc, rs, r, t_ref):
        z_ref[rs, :] = acc.astype(z_ref.dtype)
        if t_ref is not None:
            for c in range(COL_TILE // LANES):
                store_pages(t_ref, acc[:, c * LANES:(c + 1) * LANES], r, c)

    def rope_out(acc, rs, r, t_ref):
        lane = lax.broadcasted_iota(jnp.int32, (1, LANES), 1)
        first_half = (lane % HEAD_DIM) < (HEAD_DIM // 2)
        cos = cos_ref[rs, :]
        sin = sin_ref[rs, :]
        for c in range(COL_TILE // LANES):
            a = acc[:, c * LANES:(c + 1) * LANES]
            partner = jnp.where(first_half,
                                pltpu.roll(a, LANES - HEAD_DIM // 2, axis=1),
                                pltpu.roll(a, HEAD_DIM // 2, axis=1))
            rot = a * cos + partner * sin
            z_ref[rs, c * LANES:(c + 1) * LANES] = rot.astype(z_ref.dtype)
            if t_ref is not None:
                store_pages(t_ref, rot, r, c)

    def sigmoid_out(acc, rs, r, t_ref):
        z_ref[rs, :] = _sigmoid(acc).astype(z_ref.dtype)

    pl.when(j == 0)(lambda: tile(gelu_out))
    pl.when(j == 1)(lambda: tile(gelu_layernorm_out))
    pl.when(j == 2)(lambda: tile(raw_out))
    pl.when(j == 3)(lambda: tile(rope_out))
    pl.when(j == 4)(lambda: tile(rope_out, kt_ref))
    pl.when(j == 5)(lambda: tile(raw_out, vt_ref))
    pl.when(j >= GATE_COL0 // COL_TILE)(lambda: tile(sigmoid_out))


def _inproj(x2d, g_pre, w_in_bf, g_v, cos_t, sin_t, tm, emit_pages, z_dtype):
    rows = x2d.shape[0]
    t_tiles = cos_t.shape[0] // tm
    z_spec = pl.BlockSpec((tm, COL_TILE), lambda i, j: (i, j))
    z_shape = jax.ShapeDtypeStruct((rows, IN_WIDTH), z_dtype)
    if emit_pages:
        page_spec = pl.BlockSpec((tm // PAGE_SIZE, BRANCH_WIDTH, PAGE_SIZE), lambda i, j: (i, 0, 0))
        page_shape = jax.ShapeDtypeStruct((rows // PAGE_SIZE, BRANCH_WIDTH, PAGE_SIZE), F32)
        out_specs, out_shape = [z_spec, page_spec, page_spec], [z_shape, page_shape, page_shape]
    else:
        out_specs, out_shape = z_spec, z_shape
    return pl.pallas_call(
        functools.partial(_inproj_kernel, row_chunk=min(tm, 256), emit_pages=emit_pages),
        grid=(rows // tm, N_COL_TILES),
        in_specs=[
            pl.BlockSpec((tm, D_MODEL), lambda i, j: (i, 0)),
            pl.BlockSpec((1, D_MODEL), lambda i, j: (0, 0)),
            pl.BlockSpec((D_MODEL, COL_TILE), lambda i, j: (0, j)),
            pl.BlockSpec((1, COL_TILE), lambda i, j: (0, 0)),
            pl.BlockSpec((tm, LANES), lambda i, j: (i % t_tiles, 0)),
            pl.BlockSpec((tm, LANES), lambda i, j: (i % t_tiles, 0)),
        ],
        out_specs=out_specs,
        out_shape=out_shape,
        scratch_shapes=[pltpu.VMEM((tm, D_MODEL), BF16)],
        compiler_params=_cparams(2),
        name="inproj",
    )(x2d, g_pre, w_in_bf, g_v, cos_t, sin_t)


MIX_TILE = 2 * CM_CHUNK
HALO = 16


def _prompt_mix_kernel(u_ref, vn_ref, p_ref, prev_ref, ws_ref, bs_ref, wp_ref, sc_ref,
                       oab_ref, ext_ref):
    i = pl.program_id(0)
    tiles_per_seq = SEQ // MIX_TILE
    first = (i % tiles_per_seq) == 0

    row = lax.broadcasted_iota(jnp.int32, (CM_CHUNK, CM_CHUNK), 0)
    col = lax.broadcasted_iota(jnp.int32, (CM_CHUNK, CM_CHUNK), 1)
    tri = row >= col
    lane = lax.broadcasted_iota(jnp.int32, (1, LANES), 1)
    lo_half = lane < CM_GROUP_WIDTH
    w_tri = [jnp.where(tri, ws_ref[g], 0.0).astype(BF16) for g in range(CM_GROUPS)]
    for c in range(MIX_TILE // CM_CHUNK):
        rs = slice(c * CM_CHUNK, (c + 1) * CM_CHUNK)
        for pr in range(CM_GROUPS // 2):
            cs = slice(pr * LANES, (pr + 1) * LANES)
            vn = vn_ref[rs, cs].astype(F32)
            v_lo = jnp.where(lo_half, vn, 0.0).astype(BF16)
            v_hi = jnp.where(lo_half, 0.0, vn).astype(BF16)
            s = (jnp.dot(w_tri[2 * pr], v_lo, preferred_element_type=F32)
                 + jnp.dot(w_tri[2 * pr + 1], v_hi, preferred_element_type=F32))
            bias = jnp.where(lo_half, bs_ref[:, 2 * pr:2 * pr + 1], bs_ref[:, 2 * pr + 1:2 * pr + 2])
            oab_ref[rs, cs] = u_ref[rs, cs].astype(F32) * (s + bias)

    ext_ref[0:HALO, :] = jnp.where(first, 0.0, prev_ref[...].astype(F32))
    ext_ref[HALO:HALO + MIX_TILE, :] = p_ref[...].astype(F32)
    t_in_seq = (i % tiles_per_seq) * MIX_TILE + lax.broadcasted_iota(jnp.int32, (MIX_TILE, 1), 0)
    for gi, win in enumerate(POOL_WINDOWS):
        cs = slice(gi * POOL_GROUP_WIDTH, (gi + 1) * POOL_GROUP_WIDTH)
        tot = ext_ref[HALO:HALO + MIX_TILE, cs]
        for k in range(1, win):
            tot = tot + ext_ref[HALO - k:HALO - k + MIX_TILE, cs]
        cnt = jnp.minimum(t_in_seq + 1, win).astype(F32)
        d = tot / cnt - ext_ref[HALO:HALO + MIX_TILE, cs]
        y = jnp.dot(d.astype(BF16), wp_ref[gi], preferred_element_type=F32)
        oab_ref[:, BRANCH_WIDTH + gi * POOL_GROUP_WIDTH:BRANCH_WIDTH + (gi + 1) * POOL_GROUP_WIDTH] = y * sc_ref[:, cs]


def _prompt_mix(z, w_s, b_s_t, w_pool_bf, pool_scale):
    rows = z.shape[0]
    halo_blocks = MIX_TILE // HALO
    return pl.pallas_call(
        _prompt_mix_kernel,
        grid=(rows // MIX_TILE,),
        in_specs=[
            pl.BlockSpec((MIX_TILE, BRANCH_WIDTH), lambda i: (i, 0)),
            pl.BlockSpec((MIX_TILE, BRANCH_WIDTH), lambda i: (i, 1)),
            pl.BlockSpec((MIX_TILE, BRANCH_WIDTH), lambda i: (i, 2)),
            pl.BlockSpec((HALO, BRANCH_WIDTH), lambda i: (jnp.maximum(i * halo_blocks - 1, 0), 2)),
            pl.BlockSpec((CM_GROUPS, CM_CHUNK, CM_CHUNK), lambda i: (0, 0, 0)),
            pl.BlockSpec((CM_CHUNK, CM_GROUPS), lambda i: (0, 0)),
            pl.BlockSpec((len(POOL_WINDOWS), POOL_GROUP_WIDTH, POOL_GROUP_WIDTH), lambda i: (0, 0, 0)),
            pl.BlockSpec((1, BRANCH_WIDTH), lambda i: (0, 0)),
        ],
        out_specs=pl.BlockSpec((MIX_TILE, 2 * BRANCH_WIDTH), lambda i: (i, 0)),
        out_shape=jax.ShapeDtypeStruct((rows, 2 * BRANCH_WIDTH), F32),
        scratch_shapes=[pltpu.VMEM((HALO + MIX_TILE, BRANCH_WIDTH), F32)],
        compiler_params=_cparams(1),
        name="prompt_mix",
    )(z, z, z, z, w_s, b_s_t, w_pool_bf, pool_scale)


HEADS_PER_TILE = LANES // HEAD_DIM


def _top_k_rows(gate_t, blk_f, n_valid):
    n_blk = gate_t.shape[0]
    g = jnp.where(blk_f < n_valid, gate_t, NEG)
    sel = jnp.zeros(gate_t.shape, F32)
    for r in range(MOBA_TOP_K):
        mx = jnp.max(g, axis=0, keepdims=True)
        first = jnp.min(jnp.where(g == mx, blk_f, float(n_blk)), axis=0, keepdims=True)
        pick = jnp.where(blk_f == first, jnp.where(n_valid > float(r), 1.0, 0.0), 0.0)
        sel = jnp.maximum(sel, pick)
        g = jnp.where(pick > 0.0, NEG, g)
    return sel


ATTN_GROUP = 4


KM_PAGES = 16
KM_CHUNKS = (PAST_LEN // PAGE_SIZE) // KM_PAGES


def _block_means_step(chunk, page_refs, kmc_ref):
    blocks_per_step = KM_PAGES // PAGES_PER_BLOCK
    blk = lax.broadcasted_iota(jnp.int32, (1, 1, N_PAST_BLOCKS), 2)
    acc = kmc_ref[0]
    for jj in range(blocks_per_step):
        tot = page_refs[PAGES_PER_BLOCK * jj][...]
        for r in range(1, PAGES_PER_BLOCK):
            tot = tot + page_refs[PAGES_PER_BLOCK * jj + r][...]
        mean = jnp.sum(tot, axis=-1, keepdims=True) * (1.0 / MOBA_BLOCK)
        acc = jnp.where(blk == chunk * blocks_per_step + jj, mean, acc)
    kmc_ref[0] = acc


def _prompt_attn_kernel(pt_ref, q_ref, k_ref, v_ref, *rest):
    del pt_ref
    page_refs = rest[:KM_PAGES]
    o_ref, kmc_ref, kbf_ref, vt_ref, km_ref, sel_ref, s_ref, acc_ref = rest[KM_PAGES:]
    step = (pl.program_id(0) * pl.num_programs(1) + pl.program_id(1)) * pl.num_programs(2) + pl.program_id(2)
    chunk = step % KM_CHUNKS

    @pl.when(chunk == 0)
    def _():
        kmc_ref[...] = jnp.zeros_like(kmc_ref)

    _prompt_attn_body(q_ref, k_ref, v_ref, o_ref, kbf_ref, vt_ref, km_ref, sel_ref, s_ref, acc_ref,
                      lambda: _block_means_step(chunk, page_refs, kmc_ref))


def _prompt_attn_body(q_ref, k_ref, v_ref, o_ref, kbf_ref, vt_ref, km_ref, sel_ref, s_ref, acc_ref,
                      side_work):
    b = pl.program_id(2)
    nt = (((1,), (1,)), ((), ()))
    lane = lax.broadcasted_iota(jnp.int32, (1, LANES), 1)

    @pl.when(b == 0)
    def _():
        drow = lax.broadcasted_iota(jnp.int32, (LANES, MOBA_BLOCK), 0)
        for j in range(N_PROMPT_BLOCKS):
            kj = k_ref[j * MOBA_BLOCK:(j + 1) * MOBA_BLOCK, :]
            kbf_ref[j] = kj.astype(BF16)
            km_ref[j:j + 1, :] = jnp.mean(kj.astype(F32), axis=0, keepdims=True)
            vt = v_ref[j * MOBA_BLOCK:(j + 1) * MOBA_BLOCK, :].astype(F32).T
            for hh in range(HEADS_PER_TILE):
                vt_ref[hh, j] = jnp.where((drow // HEAD_DIM) == hh, vt, 1.0).astype(BF16)
        blk_f = lax.broadcasted_iota(jnp.int32, (N_PROMPT_BLOCKS, SEQ), 0).astype(F32)
        own_blk = (lax.broadcasted_iota(jnp.int32, (1, SEQ), 1) // MOBA_BLOCK).astype(F32)
        q_all = q_ref[...].astype(F32)
        for hh in range(HEADS_PER_TILE):
            qh = jnp.where((lane // HEAD_DIM) == hh, q_all, 0.0)
            gate_t = lax.dot_general(km_ref[...], qh, nt, precision=lax.Precision.HIGHEST,
                                     preferred_element_type=F32)
            sel = _top_k_rows(gate_t, blk_f, own_blk)
            for qb in range(N_PROMPT_BLOCKS):
                sel_ref[hh, qb] = sel[:, qb * MOBA_BLOCK:(qb + 1) * MOBA_BLOCK]

    side_work()
    q = (q_ref[pl.ds(pl.multiple_of(b * MOBA_BLOCK, MOBA_BLOCK), MOBA_BLOCK), :].astype(F32)
         * (HEAD_DIM ** -0.5 * LOG2_E))
    q_bf = [jnp.where((lane // HEAD_DIM) == hh, q, 0.0).astype(BF16) for hh in range(HEADS_PER_TILE)]
    key_i = lax.broadcasted_iota(jnp.int32, (MOBA_BLOCK, MOBA_BLOCK), 0)
    qry_i = lax.broadcasted_iota(jnp.int32, (MOBA_BLOCK, MOBA_BLOCK), 1)
    n_chunks = lax.shift_right_logical(b + ATTN_GROUP, ATTN_GROUP.bit_length() - 1)

    def score_chunk(c, slot, ms, first):
        ms = list(ms)
        for t in range(ATTN_GROUP):
            own = first and t == 0
            j = b if own else c * ATTN_GROUP + (t - 1)
            kj = kbf_ref[j]
            for hh in range(HEADS_PER_TILE):
                s = lax.dot_general(kj, q_bf[hh], nt, preferred_element_type=F32)
                if own:
                    s = jnp.where(key_i <= qry_i, s, NEG)
                else:
                    s = jnp.where(sel_ref[hh, b, pl.ds(j, 1), :] > 0.0, s, NEG)
                s_ref[hh, slot, t] = s
                ms[hh] = jnp.maximum(ms[hh], jnp.max(s, axis=0, keepdims=True))
        return tuple(ms)

    def value_chunk(c, slot, m_acc, m_new):
        for hh in range(HEADS_PER_TILE):
            acc = jnp.exp2(m_acc[hh] - m_new[hh]) * acc_ref[hh]
            for t in range(ATTN_GROUP):
                j = c * ATTN_GROUP + (t - 1)
                if t == 0:
                    j = jnp.where(c == 0, b, j)
                p = jnp.exp2(s_ref[hh, slot, t] - m_new[hh]).astype(BF16)
                acc = acc + jnp.dot(vt_ref[hh, j], p, preferred_element_type=F32)
            acc_ref[hh] = acc

    acc_ref[...] = jnp.zeros_like(acc_ref)
    neg_row = jnp.full((1, MOBA_BLOCK), NEG, F32)
    m_first = score_chunk(0, 0, (neg_row,) * HEADS_PER_TILE, True)

    def pipelined(c, carry):
        m_acc, m_prev = carry
        value_chunk(c - 1, (c - 1) & 1, m_acc, m_prev)
        m_cur = score_chunk(c, c & 1, m_prev, False)
        return m_prev, m_cur

    m_acc, m_last = lax.fori_loop(1, n_chunks, pipelined, (m_first, m_first))
    value_chunk(n_chunks - 1, (n_chunks - 1) & 1, m_acc, m_last)

    drow = lax.broadcasted_iota(jnp.int32, (LANES, MOBA_BLOCK), 0)
    a0 = acc_ref[0]
    a1 = acc_ref[1]
    out_t = jnp.where(drow < HEAD_DIM, a0 / a0[HEAD_DIM:HEAD_DIM + 1, :], a1 / a1[0:1, :])
    o_ref[...] = out_t.T


def _prompt_attn(z, cache_kt, page_table, layer):
    rows = z.shape[0]
    q0 = 3 * BRANCH_WIDTH // LANES
    k0 = 4 * BRANCH_WIDTH // LANES
    v0 = 5 * BRANCH_WIDTH // LANES
    n_tiles = BRANCH_WIDTH // LANES
    assert HEADS_PER_TILE == 2 and N_PROMPT_BLOCKS % ATTN_GROUP == 0
    assert BATCH * n_tiles * N_PROMPT_BLOCKS == DEC_BATCH * KM_CHUNKS

    def step(n, hp, b):
        return (n * n_tiles + hp) * N_PROMPT_BLOCKS + b

    def page_spec(r):
        def index_map(n, hp, b, pt):
            s = step(n, hp, b)
            return (layer, pt[s // KM_CHUNKS, (s % KM_CHUNKS) * KM_PAGES + r], 0, 0, 0)
        return pl.BlockSpec((None, None, N_HEADS, HEAD_DIM, PAGE_SIZE), index_map)

    return pl.pallas_call(
        _prompt_attn_kernel,
        grid_spec=pltpu.PrefetchScalarGridSpec(
            num_scalar_prefetch=1,
            grid=(BATCH, n_tiles, N_PROMPT_BLOCKS),
            in_specs=[
                pl.BlockSpec((SEQ, LANES), lambda n, hp, b, pt: (n, q0 + hp)),
                pl.BlockSpec((SEQ, LANES), lambda n, hp, b, pt: (n, k0 + hp)),
                pl.BlockSpec((SEQ, LANES), lambda n, hp, b, pt: (n, v0 + hp)),
            ] + [page_spec(r) for r in range(KM_PAGES)],
            out_specs=[
                pl.BlockSpec((MOBA_BLOCK, LANES), lambda n, hp, b, pt: (n * N_PROMPT_BLOCKS + b, hp)),
                pl.BlockSpec((1, N_HEADS, HEAD_DIM, N_PAST_BLOCKS),
                             lambda n, hp, b, pt: (step(n, hp, b) // KM_CHUNKS, 0, 0, 0)),
            ],
            scratch_shapes=[pltpu.VMEM((N_PROMPT_BLOCKS, MOBA_BLOCK, LANES), BF16),
                            pltpu.VMEM((HEADS_PER_TILE, N_PROMPT_BLOCKS, LANES, MOBA_BLOCK), BF16),
                            pltpu.VMEM((N_PROMPT_BLOCKS, LANES), F32),
                            pltpu.VMEM((HEADS_PER_TILE, N_PROMPT_BLOCKS, N_PROMPT_BLOCKS, MOBA_BLOCK), F32),
                            pltpu.VMEM((HEADS_PER_TILE, 2, ATTN_GROUP, MOBA_BLOCK, MOBA_BLOCK), F32),
                            pltpu.VMEM((HEADS_PER_TILE, LANES, MOBA_BLOCK), F32)],
        ),
        out_shape=[jax.ShapeDtypeStruct((rows, BRANCH_WIDTH), F32),
                   jax.ShapeDtypeStruct((DEC_BATCH, N_HEADS, HEAD_DIM, N_PAST_BLOCKS), F32)],
        compiler_params=_cparams(3),
        name="prompt_attn",
    )(page_table, z, z, z, *([cache_kt] * KM_PAGES))


def _finish_kernel(x_ref, oab_ref, oc_ref, gates_ref, wb_ref, wo_ref, g_ref, o_ref):
    br_a = jnp.dot(oab_ref[:, 0:BRANCH_WIDTH].astype(BF16), wb_ref[0], preferred_element_type=F32)
    br_b = jnp.dot(oab_ref[:, BRANCH_WIDTH:2 * BRANCH_WIDTH].astype(BF16), wb_ref[1], preferred_element_type=F32)
    br_c = jnp.dot(oc_ref[...].astype(BF16), wb_ref[2], preferred_element_type=F32)
    mixed = (gates_ref[:, 0:D_MODEL] * br_a + gates_ref[:, D_MODEL:2 * D_MODEL] * br_b
             + gates_ref[:, 2 * D_MODEL:3 * D_MODEL] * br_c)
    mix = jnp.dot(mixed.astype(BF16), wo_ref[...], preferred_element_type=F32)
    o_ref[...] = x_ref[...] + _rms(mix, g_ref[...])


def _finish(x2d, oab, oc, z, w_branch_bf, w_out_bf, g_post_mix, tm):
    rows = x2d.shape[0]
    return pl.pallas_call(
        _finish_kernel,
        grid=(rows // tm,),
        in_specs=[
            pl.BlockSpec((tm, D_MODEL), lambda i: (i, 0)),
            pl.BlockSpec((tm, 2 * BRANCH_WIDTH), lambda i: (i, 0)),
            pl.BlockSpec((tm, BRANCH_WIDTH), lambda i: (i, 0)),
            pl.BlockSpec((tm, N_BRANCH * D_MODEL), lambda i: (i, 1)),
            pl.BlockSpec((N_BRANCH, BRANCH_WIDTH, D_MODEL), lambda i: (0, 0, 0)),
            pl.BlockSpec((D_MODEL, D_MODEL), lambda i: (0, 0)),
            pl.BlockSpec((1, D_MODEL), lambda i: (0, 0)),
        ],
        out_specs=pl.BlockSpec((tm, D_MODEL), lambda i: (i, 0)),
        out_shape=jax.ShapeDtypeStruct((rows, D_MODEL), F32),
        compiler_params=_cparams(1),
        name="finish",
    )(x2d, oab, oc, z, w_branch_bf, w_out_bf, g_post_mix)


def _ffn_kernel(x_ref, gpre_ref, wg_ref, wu_ref, wo_ref, gpost_ref, o_ref, xn_ref, acc_ref):
    f = pl.program_id(1)

    @pl.when(f == 0)
    def _():
        xn_ref[...] = _rms(x_ref[...], gpre_ref[...]).astype(BF16)
        acc_ref[...] = jnp.zeros_like(acc_ref)

    xn = xn_ref[...]
    gt = jnp.dot(xn, wg_ref[...], preferred_element_type=F32)
    up = jnp.dot(xn, wu_ref[...], preferred_element_type=F32)
    h = (gt * _sigmoid(gt)) * up
    acc_ref[...] += jnp.dot(h.astype(BF16), wo_ref[...], preferred_element_type=F32)

    @pl.when(f == pl.num_programs(1) - 1)
    def _():
        o_ref[...] = x_ref[...] + _rms(acc_ref[...], gpost_ref[...])


def _ffn(x2d, g_pre, w_ff_in_bf, w_ff_out_bf, g_post, tm):
    rows = x2d.shape[0]
    n_f = D_FF // FF_TILE
    return pl.pallas_call(
        _ffn_kernel,
        grid=(rows // tm, n_f),
        in_specs=[
            pl.BlockSpec((tm, D_MODEL), lambda i, f: (i, 0)),
            pl.BlockSpec((1, D_MODEL), lambda i, f: (0, 0)),
            pl.BlockSpec((D_MODEL, FF_TILE), lambda i, f: (0, f)),
            pl.BlockSpec((D_MODEL, FF_TILE), lambda i, f: (0, n_f + f)),
            pl.BlockSpec((FF_TILE, D_MODEL), lambda i, f: (f, 0)),
            pl.BlockSpec((1, D_MODEL), lambda i, f: (0, 0)),
        ],
        out_specs=pl.BlockSpec((tm, D_MODEL), lambda i, f: (i, 0)),
        out_shape=jax.ShapeDtypeStruct((rows, D_MODEL), F32),
        scratch_shapes=[pltpu.VMEM((tm, D_MODEL), BF16), pltpu.VMEM((tm, D_MODEL), F32)],
        compiler_params=_cparams(2),
        name="ffn",
    )(x2d, g_pre, w_ff_in_bf, w_ff_in_bf, w_ff_out_bf, g_post)


SAMPLE_ROWS = DEC_BATCH * DEC_SEQ


def _sample_mix_kernel(u_ref, vn_ref, p_ref, st_ref, wv_ref, bv_ref, wp_ref, sc_ref, oab_ref):
    def slab(ref, s):
        return ref[s * DEC_BATCH:(s + 1) * DEC_BATCH, :]

    for t in range(DEC_SEQ):
        s_t = bv_ref[t:t + 1, :]
        for sp in range(t + 1):
            s_t = s_t + wv_ref[t, sp:sp + 1, :] * slab(vn_ref, sp)
        oab_ref[t * DEC_BATCH:(t + 1) * DEC_BATCH, 0:BRANCH_WIDTH] = slab(u_ref, t) * s_t

    ext = [st_ref[r] for r in range(POOL_KEEP)] + [slab(p_ref, s) for s in range(DEC_SEQ)]
    lane = lax.broadcasted_iota(jnp.int32, (1, BRANCH_WIDTH), 1)
    d_rows = []
    for s in range(DEC_SEQ):
        end = POOL_KEEP + s + 1
        mean = jnp.zeros((DEC_BATCH, BRANCH_WIDTH), F32)
        run = jnp.zeros((DEC_BATCH, BRANCH_WIDTH), F32)
        taken = 0
        for gi, win in enumerate(POOL_WINDOWS):
            for r in range(end - win, end - taken):
                run = run + ext[r]
            taken = win
            mean = jnp.where((lane // POOL_GROUP_WIDTH) == gi, run / float(win), mean)
        d_rows.append(mean - ext[POOL_KEEP + s])
    d = jnp.concatenate(d_rows, axis=0).astype(BF16)
    for gi in range(len(POOL_WINDOWS)):
        cs = slice(gi * POOL_GROUP_WIDTH, (gi + 1) * POOL_GROUP_WIDTH)
        y = jnp.dot(d[:, cs], wp_ref[gi], preferred_element_type=F32)
        oab_ref[:, BRANCH_WIDTH + gi * POOL_GROUP_WIDTH:BRANCH_WIDTH + (gi + 1) * POOL_GROUP_WIDTH] = y * sc_ref[:, cs]


def _sample_mix(z, state_t, w_vec, b_vec, w_pool_bf, pool_scale):
    return pl.pallas_call(
        _sample_mix_kernel,
        grid=(1,),
        in_specs=[
            pl.BlockSpec((SAMPLE_ROWS, BRANCH_WIDTH), lambda i: (0, 0)),
            pl.BlockSpec((SAMPLE_ROWS, BRANCH_WIDTH), lambda i: (0, 1)),
            pl.BlockSpec((SAMPLE_ROWS, BRANCH_WIDTH), lambda i: (0, 2)),
            pl.BlockSpec((POOL_KEEP, DEC_BATCH, BRANCH_WIDTH), lambda i: (0, 0, 0)),
            pl.BlockSpec((DEC_SEQ, DEC_SEQ, BRANCH_WIDTH), lambda i: (0, 0, 0)),
            pl.BlockSpec((DEC_SEQ, BRANCH_WIDTH), lambda i: (0, 0)),
            pl.BlockSpec((len(POOL_WINDOWS), POOL_GROUP_WIDTH, POOL_GROUP_WIDTH), lambda i: (0, 0, 0)),
            pl.BlockSpec((1, BRANCH_WIDTH), lambda i: (0, 0)),
        ],
        out_specs=pl.BlockSpec((SAMPLE_ROWS, 2 * BRANCH_WIDTH), lambda i: (0, 0)),
        out_shape=jax.ShapeDtypeStruct((SAMPLE_ROWS, 2 * BRANCH_WIDTH), F32),
        compiler_params=_cparams(1),
        name="sample_mix",
    )(z, z, z, state_t, w_vec, b_vec, w_pool_bf, pool_scale)


def _gate_topk_kernel(q_ref, km_ref, idx_ref):
    lane_f = lax.broadcasted_iota(jnp.int32, (DEC_SEQ, N_PAST_BLOCKS), 1).astype(F32)
    for h in range(N_HEADS):
        gate = jnp.dot(q_ref[0, h], km_ref[0, h], precision=lax.Precision.HIGHEST,
                       preferred_element_type=F32)
        g = gate
        for r in range(MOBA_TOP_K):
            mx = jnp.max(g, axis=-1, keepdims=True)
            first = jnp.min(jnp.where(g == mx, lane_f, float(N_PAST_BLOCKS)), axis=-1, keepdims=True)
            idx_ref[0, h, :, r:r + 1] = first.astype(jnp.int32)
            g = jnp.where(lane_f == first, NEG, g)


def _gate_topk(q_nh, km_nh):
    return pl.pallas_call(
        _gate_topk_kernel,
        grid=(DEC_BATCH,),
        in_specs=[
            pl.BlockSpec((1, N_HEADS, DEC_SEQ, HEAD_DIM), lambda n: (n, 0, 0, 0)),
            pl.BlockSpec((1, N_HEADS, HEAD_DIM, N_PAST_BLOCKS), lambda n: (n, 0, 0, 0)),
        ],
        out_specs=pl.BlockSpec((1, N_HEADS, DEC_SEQ, MOBA_TOP_K), lambda n: (n, 0, 0, 0)),
        out_shape=jax.ShapeDtypeStruct((DEC_BATCH, N_HEADS, DEC_SEQ, MOBA_TOP_K), jnp.int32),
        compiler_params=_cparams(1),
        name="gate_topk",
    )(q_nh, km_nh)


N_SEL_PAGES = MOBA_TOP_K * PAGES_PER_BLOCK


N_SLABS = DEC_SEQ * N_SEL_PAGES


def _sample_attn_kernel(pt_ref, idx_ref, q_ref, kn_ref, vn_ref, ck_ref, cv_ref, o_ref,
                        kbuf, vbuf, sem, *, layer):
    n = pl.program_id(0)
    slot = n & 1

    def slab_copies(seq, buf_slot, h, i):
        s, r = divmod(i, N_SEL_PAGES)
        blk = idx_ref[((seq * N_HEADS + h) * DEC_SEQ + s) * MOBA_TOP_K + r // PAGES_PER_BLOCK]
        page = pt_ref[seq, blk * PAGES_PER_BLOCK + r % PAGES_PER_BLOCK]
        return (pltpu.make_async_copy(ck_ref.at[layer, page, h], kbuf.at[buf_slot, h, i], sem.at[buf_slot]),
                pltpu.make_async_copy(cv_ref.at[layer, page, h], vbuf.at[buf_slot, h, i], sem.at[buf_slot]))

    def for_each_slab(seq, buf_slot, fn):
        def per_head(h, carry):
            for i in range(N_SLABS):
                for cp in slab_copies(seq, buf_slot, h, i):
                    fn(cp)
            return carry
        lax.fori_loop(0, N_HEADS, per_head, 0)

    @pl.when(n == 0)
    def _():
        for_each_slab(n, slot, lambda cp: cp.start())

    @pl.when(n + 1 < pl.num_programs(0))
    def _():
        for_each_slab(n + 1, 1 - slot, lambda cp: cp.start())

    for_each_slab(n, slot, lambda cp: cp.wait())

    scale = HEAD_DIM ** -0.5
    pos = lax.broadcasted_iota(jnp.int32, (1, DEC_SEQ), 1)

    def per_head(h, carry):
        k_new = kn_ref[0, h]
        v_new = vn_ref[0, h]
        for s in range(DEC_SEQ):
            q = q_ref[0, h, :, s:s + 1] * scale
            s_sel = [jnp.sum(q * kbuf[slot, h, s * N_SEL_PAGES + r], axis=0, keepdims=True)
                     for r in range(N_SEL_PAGES)]
            s_own = jnp.where(pos <= s, jnp.sum(q * k_new, axis=0, keepdims=True), NEG)
            m_sel = functools.reduce(jnp.maximum, s_sel)
            m = jnp.maximum(jnp.max(m_sel, axis=-1, keepdims=True), jnp.max(s_own, axis=-1, keepdims=True))
            p_sel = [jnp.exp(sr - m) for sr in s_sel]
            p_own = jnp.exp(s_own - m)
            l = (jnp.sum(functools.reduce(jnp.add, p_sel), axis=-1, keepdims=True)
                 + jnp.sum(p_own, axis=-1, keepdims=True))
            pv = functools.reduce(jnp.add, [vbuf[slot, h, s * N_SEL_PAGES + r] * p_sel[r]
                                            for r in range(N_SEL_PAGES)])
            o = jnp.sum(pv, axis=-1, keepdims=True) + jnp.sum(v_new * p_own, axis=-1, keepdims=True)
            o_ref[0, h, :, s:s + 1] = o / l
        return carry

    lax.fori_loop(0, N_HEADS, per_head, 0)


def _sample_attn(q_cols, k_cols, v_cols, cache_kt, cache_vt, page_table, idx, layer):
    def new_spec():
        return pl.BlockSpec((1, N_HEADS, HEAD_DIM, DEC_SEQ), lambda n, pt, ix: (n, 0, 0, 0))

    buf = pltpu.VMEM((2, N_HEADS, N_SLABS, HEAD_DIM, PAGE_SIZE), F32)
    return pl.pallas_call(
        functools.partial(_sample_attn_kernel, layer=layer),
        grid_spec=pltpu.PrefetchScalarGridSpec(
            num_scalar_prefetch=2,
            grid=(DEC_BATCH,),
            in_specs=[new_spec(), new_spec(), new_spec(),
                      pl.BlockSpec(memory_space=pl.ANY), pl.BlockSpec(memory_space=pl.ANY)],
            out_specs=new_spec(),
            scratch_shapes=[buf, buf, pltpu.SemaphoreType.DMA((2,))],
        ),
        out_shape=jax.ShapeDtypeStruct((DEC_BATCH, N_HEADS, HEAD_DIM, DEC_SEQ), F32),
        compiler_params=_cparams(1),
        name="sample_attn",
    )(page_table, idx.reshape(-1), q_cols, k_cols, v_cols, cache_kt, cache_vt)


def _rope_tables(pos):
    half = HEAD_DIM // 2
    inv = ROPE_THETA ** (-jnp.arange(half, dtype=F32) / half)
    ang = pos.astype(F32)[:, None] * inv[None, :]
    cos, sin = jnp.cos(ang), jnp.sin(ang)
    reps = LANES // half
    cos_t = jnp.tile(cos, (1, reps))
    sin_t = jnp.concatenate([-sin, sin] * (reps // 2), axis=1)
    return cos_t, sin_t


def _pages_out(per_layer):
    t = jnp.stack(per_layer).reshape(DEPTH, BATCH, SEQ // PAGE_SIZE, N_HEADS, HEAD_DIM, PAGE_SIZE)
    return t.transpose(0, 1, 2, 3, 5, 4)


def _sample_heads(a2d):
    return a2d.reshape(DEC_SEQ, DEC_BATCH, N_HEADS, HEAD_DIM).transpose(1, 2, 0, 3)


def _sample_rows(a2d):
    return a2d.reshape(DEC_SEQ, DEC_BATCH, a2d.shape[-1]).transpose(1, 0, 2)


def kernel(x_prompt, x_sample, cache_k, cache_v, state_pool, page_table, w_in, g_v, w_s, b_s, w_pool, pool_scale, w_branch, w_out, g_pre_mix, g_post_mix, g_pre_ffn, g_post_ffn, w_ff_in, w_ff_out):
    cos_p, sin_p = _rope_tables(jnp.arange(SEQ))
    cos_s, sin_s = _rope_tables(PAST_LEN + jnp.arange(SAMPLE_ROWS) // DEC_BATCH)

    xp = x_prompt.reshape(BATCH * SEQ, D_MODEL)
    xs = x_sample.transpose(1, 0, 2).reshape(SAMPLE_ROWS, D_MODEL)
    page_table = page_table.astype(jnp.int32)
    cache_kt = cache_k.transpose(0, 1, 2, 4, 3)
    cache_vt = cache_v.transpose(0, 1, 2, 4, 3)

    kp_l, vp_l, poolp_l, ks_l, vs_l, pools_l, cvs_l = [], [], [], [], [], [], []
    for l in range(DEPTH):
        w_in_bf = w_in[l].astype(BF16)
        w_pool_bf = w_pool[l].astype(BF16)
        w_branch_bf = w_branch[l].astype(BF16)
        w_out_bf = w_out[l].astype(BF16)
        w_ff_in_bf = w_ff_in[l].astype(BF16)
        w_ff_out_bf = w_ff_out[l].astype(BF16)
        row = lambda a: a[l].reshape(1, -1)
        b_s_t = b_s[l].T

        zp, kt_pages, vt_pages = _inproj(xp, row(g_pre_mix), w_in_bf, row(g_v), cos_p, sin_p,
                                         tm=1024, emit_pages=True, z_dtype=BF16)
        oab = _prompt_mix(zp, w_s[l], b_s_t, w_pool_bf, row(pool_scale))
        oc, km = _prompt_attn(zp, cache_kt, page_table, l)
        x1 = _finish(xp, oab, oc, zp, w_branch_bf, w_out_bf, row(g_post_mix), tm=256)
        xp = _ffn(x1, row(g_pre_ffn), w_ff_in_bf, w_ff_out_bf, row(g_post_ffn), tm=512)
        kp_l.append(kt_pages)
        vp_l.append(vt_pages)
        poolp_l.append(zp[:, 2 * BRANCH_WIDTH:3 * BRANCH_WIDTH]
                       .reshape(BATCH, SEQ, BRANCH_WIDTH)[:, SEQ - POOL_KEEP:].astype(F32))

        zs = _inproj(xs, row(g_pre_mix), w_in_bf, row(g_v), cos_s, sin_s, tm=SAMPLE_ROWS,
                     emit_pages=False, z_dtype=F32)
        w_vec = jnp.repeat(w_s[l][:, :DEC_SEQ, :DEC_SEQ].transpose(1, 2, 0), CM_GROUP_WIDTH, axis=-1)
        b_vec = jnp.repeat(b_s[l][:, :DEC_SEQ].T, CM_GROUP_WIDTH, axis=-1)
        state_t = state_pool[l].transpose(1, 0, 2)
        oab_s = _sample_mix(zs, state_t, w_vec, b_vec, w_pool_bf, row(pool_scale))
        q_nh = _sample_heads(zs[:, 3 * BRANCH_WIDTH:4 * BRANCH_WIDTH])
        k_nh = _sample_heads(zs[:, 4 * BRANCH_WIDTH:5 * BRANCH_WIDTH])
        v_nh = _sample_heads(zs[:, 5 * BRANCH_WIDTH:6 * BRANCH_WIDTH])
        idx = _gate_topk(q_nh, km)
        to_cols = lambda a: a.transpose(0, 1, 3, 2)
        oc_cols = _sample_attn(to_cols(q_nh), to_cols(k_nh), to_cols(v_nh), cache_kt, cache_vt,
                               page_table, idx, l)
        oc_s = oc_cols.transpose(3, 0, 1, 2).reshape(SAMPLE_ROWS, BRANCH_WIDTH)
        x1s = _finish(xs, oab_s, oc_s, zs, w_branch_bf, w_out_bf, row(g_post_mix), tm=SAMPLE_ROWS)
        xs = _ffn(x1s, row(g_pre_ffn), w_ff_in_bf, w_ff_out_bf, row(g_post_ffn), tm=SAMPLE_ROWS)
        ks_l.append(k_nh)
        vs_l.append(v_nh)
        p_s = _sample_rows(zs[:, 2 * BRANCH_WIDTH:3 * BRANCH_WIDTH])
        pools_l.append(jnp.concatenate([state_pool[l][:, DEC_SEQ:], p_s], axis=1))
        cvs_l.append(_sample_rows(zs[:, BRANCH_WIDTH:2 * BRANCH_WIDTH]))

    y_prompt = xp.reshape(BATCH, SEQ, D_MODEL)
    y_sample = _sample_rows(xs)
    return (y_prompt, y_sample, _pages_out(kp_l), _pages_out(vp_l), jnp.stack(poolp_l),
            jnp.stack(ks_l), jnp.stack(vs_l), jnp.stack(pools_l), jnp.stack(cvs_l))
```

```python
import functools

import numpy as np
import jax
import jax.numpy as jnp
from jax import lax
from jax.experimental import pallas as pl
from jax.experimental.pallas import tpu as pltpu

D_MODEL = 1024
BATCH = 4
SEQ = 4096
DEPTH = 2
DEC_BATCH = 32
DEC_SEQ = 4
PAST_LEN = 16384
PAGE_SIZE = 128

BRANCH_WIDTH = D_MODEL // 2
HEAD_DIM = 64
N_HEADS = BRANCH_WIDTH // HEAD_DIM
MOBA_BLOCK = 256
MOBA_TOP_K = 3
ROPE_THETA = 10000.0
CM_CHUNK = 128
CM_GROUPS = 8
CM_GROUP_WIDTH = BRANCH_WIDTH // CM_GROUPS
POOL_WINDOWS = (2, 4, 8, 16)
POOL_GROUP_WIDTH = BRANCH_WIDTH // len(POOL_WINDOWS)
POOL_KEEP = max(POOL_WINDOWS) - 1
N_BRANCH = 3
D_FF = -(-8 * D_MODEL // (3 * 256)) * 256
IN_WIDTH = 6 * BRANCH_WIDTH + N_BRANCH * D_MODEL
NORM_EPS = 1e-6
NEG = -1e30
LOG2_E = 1.4426950408889634

LANES = 128
COL_TILE = BRANCH_WIDTH
N_COL_TILES = IN_WIDTH // COL_TILE
GATE_COL0 = 6 * BRANCH_WIDTH
PAGES_PER_BLOCK = MOBA_BLOCK // PAGE_SIZE
N_PAST_BLOCKS = PAST_LEN // MOBA_BLOCK
N_PROMPT_BLOCKS = SEQ // MOBA_BLOCK
FF_TILE = D_FF // 2
VMEM_LIMIT = 56 * 1024 * 1024

F32 = jnp.float32
BF16 = jnp.bfloat16


def _cparams(n_axes):
    return pltpu.CompilerParams(dimension_semantics=("arbitrary",) * n_axes,
                                vmem_limit_bytes=VMEM_LIMIT)


def _rms(x, g):
    return (x * lax.rsqrt(jnp.mean(x * x, axis=-1, keepdims=True) + NORM_EPS)) * g


def _sigmoid(x):
    return 1.0 / (1.0 + jnp.exp(-x))


def _inproj_kernel(x_ref, g_ref, w_ref, gv_ref, cos_ref, sin_ref, z_ref, *rest, row_chunk, emit_pages):
    xn_ref = rest[-1]
    kt_ref, vt_ref = rest[:2] if emit_pages else (None, None)
    j = pl.program_id(1)
    tm = x_ref.shape[0]
    pages_per_chunk = row_chunk // PAGE_SIZE

    @pl.when(j == 0)
    def _():
        xn_ref[...] = _rms(x_ref[...], g_ref[...]).astype(BF16)

    def tile(epilogue, t_ref=None):
        for r in range(tm // row_chunk):
            rs = slice(r * row_chunk, (r + 1) * row_chunk)
            acc = jnp.dot(xn_ref[rs, :], w_ref[...], preferred_element_type=F32)
            epilogue(acc, rs, r, t_ref)

    def store_pages(t_ref, vals, r, c):
        for pg in range(pages_per_chunk):
            t_ref[r * pages_per_chunk + pg, c * LANES:(c + 1) * LANES, :] = (
                vals[pg * PAGE_SIZE:(pg + 1) * PAGE_SIZE, :].T)

    def gelu_out(acc, rs, r, t_ref):
        z_ref[rs, :] = jax.nn.gelu(acc).astype(z_ref.dtype)

    def gelu_layernorm_out(acc, rs, r, t_ref):
        v = jax.nn.gelu(acc)
        vc = v - jnp.mean(v, axis=-1, keepdims=True)
        y = vc * lax.rsqrt(jnp.mean(vc * vc, axis=-1, keepdims=True) + NORM_EPS)
        z_ref[rs, :] = (y * gv_ref[...]).astype(z_ref.dtype)

    def raw_out(acc, rs, r, t_ref):
        z_ref[rs, :] = acc.astype(z_ref.dtype)
        if t_ref is not None:
            for c in range(COL_TILE // LANES):
                store_pages(t_ref, acc[:, c * LANES:(c + 1) * LANES], r, c)

    def rope_out(acc, rs, r, t_ref):
        lane = lax.broadcasted_iota(jnp.int32, (1, LANES), 1)
        first_half = (lane % HEAD_DIM) < (HEAD_DIM // 2)
        cos = cos_ref[rs, :]
        sin = sin_ref[rs, :]
        for c in range(COL_TILE // LANES):
            a = acc[:, c * LANES:(c + 1) * LANES]
            partner = jnp.where(first_half,
                                pltpu.roll(a, LANES - HEAD_DIM // 2, axis=1),
                                pltpu.roll(a, HEAD_DIM // 2, axis=1))
            rot = a * cos + partner * sin
            z_ref[rs, c * LANES:(c + 1) * LANES] = rot.astype(z_ref.dtype)
            if t_ref is not None:
                store_pages(t_ref, rot, r, c)

    def sigmoid_out(acc, rs, r, t_ref):
        z_ref[rs, :] = _sigmoid(acc).astype(z_ref.dtype)

    pl.when(j == 0)(lambda: tile(gelu_out))
    pl.when(j == 1)(lambda: tile(gelu_layernorm_out))
    pl.when(j == 2)(lambda: tile(raw_out))
    pl.when(j == 3)(lambda: tile(rope_out))
    pl.when(j == 4)(lambda: tile(rope_out, kt_ref))
    pl.when(j == 5)(lambda: tile(raw_out, vt_ref))
    pl.when(j >= GATE_COL0 // COL_TILE)(lambda: tile(sigmoid_out))


def _inproj(x2d, g_pre, w_in_bf, g_v, cos_t, sin_t, tm, emit_pages, z_dtype):
    rows = x2d.shape[0]
    t_tiles = cos_t.shape[0] // tm
    z_spec = pl.BlockSpec((tm, COL_TILE), lambda i, j: (i, j))
    z_shape = jax.ShapeDtypeStruct((rows, IN_WIDTH), z_dtype)
    if emit_pages:
        page_spec = pl.BlockSpec((tm // PAGE_SIZE, BRANCH_WIDTH, PAGE_SIZE), lambda i, j: (i, 0, 0))
        page_shape = jax.ShapeDtypeStruct((rows // PAGE_SIZE, BRANCH_WIDTH, PAGE_SIZE), F32)
        out_specs, out_shape = [z_spec, page_spec, page_spec], [z_shape, page_shape, page_shape]
    else:
        out_specs, out_shape = z_spec, z_shape
    return pl.pallas_call(
        functools.partial(_inproj_kernel, row_chunk=min(tm, 256), emit_pages=emit_pages),
        grid=(rows // tm, N_COL_TILES),
        in_specs=[
            pl.BlockSpec((tm, D_MODEL), lambda i, j: (i, 0)),
            pl.BlockSpec((1, D_MODEL), lambda i, j: (0, 0)),
            pl.BlockSpec((D_MODEL, COL_TILE), lambda i, j: (0, j)),
            pl.BlockSpec((1, COL_TILE), lambda i, j: (0, 0)),
            pl.BlockSpec((tm, LANES), lambda i, j: (i % t_tiles, 0)),
            pl.BlockSpec((tm, LANES), lambda i, j: (i % t_tiles, 0)),
        ],
        out_specs=out_specs,
        out_shape=out_shape,
        scratch_shapes=[pltpu.VMEM((tm, D_MODEL), BF16)],
        compiler_params=_cparams(2),
        name="inproj",
    )(x2d, g_pre, w_in_bf, g_v, cos_t, sin_t)


MIX_TILE = 2 * CM_CHUNK
HALO = 16


def _prompt_mix_kernel(u_ref, vn_ref, p_ref, prev_ref, ws_ref, bs_ref, wp_ref, sc_ref,
                       oab_ref, ext_ref):
    i = pl.program_id(0)
    tiles_per_seq = SEQ // MIX_TILE
    first = (i % tiles_per_seq) == 0

    row = lax.broadcasted_iota(jnp.int32, (CM_CHUNK, CM_CHUNK), 0)
    col = lax.broadcasted_iota(jnp.int32, (CM_CHUNK, CM_CHUNK), 1)
    tri = row >= col
    lane = lax.broadcasted_iota(jnp.int32, (1, LANES), 1)
    lo_half = lane < CM_GROUP_WIDTH
    w_tri = [jnp.where(tri, ws_ref[g], 0.0).astype(BF16) for g in range(CM_GROUPS)]
    for c in range(MIX_TILE // CM_CHUNK):
        rs = slice(c * CM_CHUNK, (c + 1) * CM_CHUNK)
        for pr in range(CM_GROUPS // 2):
            cs = slice(pr * LANES, (pr + 1) * LANES)
            vn = vn_ref[rs, cs].astype(F32)
            v_lo = jnp.where(lo_half, vn, 0.0).astype(BF16)
            v_hi = jnp.where(lo_half, 0.0, vn).astype(BF16)
            s = (jnp.dot(w_tri[2 * pr], v_lo, preferred_element_type=F32)
                 + jnp.dot(w_tri[2 * pr + 1], v_hi, preferred_element_type=F32))
            bias = jnp.where(lo_half, bs_ref[:, 2 * pr:2 * pr + 1], bs_ref[:, 2 * pr + 1:2 * pr + 2])
            oab_ref[rs, cs] = (u_ref[rs, cs].astype(F32) * (s + bias)).astype(oab_ref.dtype)

    ext_ref[0:HALO, :] = jnp.where(first, 0.0, prev_ref[...].astype(F32))
    ext_ref[HALO:HALO + MIX_TILE, :] = p_ref[...].astype(F32)
    t_in_seq = (i % tiles_per_seq) * MIX_TILE + lax.broadcasted_iota(jnp.int32, (MIX_TILE, 1), 0)
    for gi, win in enumerate(POOL_WINDOWS):
        cs = slice(gi * POOL_GROUP_WIDTH, (gi + 1) * POOL_GROUP_WIDTH)
        tot = ext_ref[HALO:HALO + MIX_TILE, cs]
        for k in range(1, win):
            tot = tot + ext_ref[HALO - k:HALO - k + MIX_TILE, cs]
        cnt = jnp.minimum(t_in_seq + 1, win).astype(F32)
        d = tot / cnt - ext_ref[HALO:HALO + MIX_TILE, cs]
        y = jnp.dot(d.astype(BF16), wp_ref[gi], preferred_element_type=F32)
        oab_ref[:, BRANCH_WIDTH + gi * POOL_GROUP_WIDTH:BRANCH_WIDTH + (gi + 1) * POOL_GROUP_WIDTH] = (
            y * sc_ref[:, cs]).astype(oab_ref.dtype)


def _prompt_mix(z, w_s, b_s_t, w_pool_bf, pool_scale):
    rows = z.shape[0]
    halo_blocks = MIX_TILE // HALO
    return pl.pallas_call(
        _prompt_mix_kernel,
        grid=(rows // MIX_TILE,),
        in_specs=[
            pl.BlockSpec((MIX_TILE, BRANCH_WIDTH), lambda i: (i, 0)),
            pl.BlockSpec((MIX_TILE, BRANCH_WIDTH), lambda i: (i, 1)),
            pl.BlockSpec((MIX_TILE, BRANCH_WIDTH), lambda i: (i, 2)),
            pl.BlockSpec((HALO, BRANCH_WIDTH), lambda i: (jnp.maximum(i * halo_blocks - 1, 0), 2)),
            pl.BlockSpec((CM_GROUPS, CM_CHUNK, CM_CHUNK), lambda i: (0, 0, 0)),
            pl.BlockSpec((CM_CHUNK, CM_GROUPS), lambda i: (0, 0)),
            pl.BlockSpec((len(POOL_WINDOWS), POOL_GROUP_WIDTH, POOL_GROUP_WIDTH), lambda i: (0, 0, 0)),
            pl.BlockSpec((1, BRANCH_WIDTH), lambda i: (0, 0)),
        ],
        out_specs=pl.BlockSpec((MIX_TILE, 2 * BRANCH_WIDTH), lambda i: (i, 0)),
        out_shape=jax.ShapeDtypeStruct((rows, 2 * BRANCH_WIDTH), BF16),
        scratch_shapes=[pltpu.VMEM((HALO + MIX_TILE, BRANCH_WIDTH), F32)],
        compiler_params=_cparams(1),
        name="prompt_mix",
    )(z, z, z, z, w_s, b_s_t, w_pool_bf, pool_scale)


HEADS_PER_TILE = LANES // HEAD_DIM


def _top_k_rows(gate_t, blk_f, n_valid):
    n_blk = gate_t.shape[0]
    g = jnp.where(blk_f < n_valid, gate_t, NEG)
    sel = jnp.zeros(gate_t.shape, F32)
    for r in range(MOBA_TOP_K):
        mx = jnp.max(g, axis=0, keepdims=True)
        first = jnp.min(jnp.where(g == mx, blk_f, float(n_blk)), axis=0, keepdims=True)
        pick = jnp.where(blk_f == first, jnp.where(n_valid > float(r), 1.0, 0.0), 0.0)
        sel = jnp.maximum(sel, pick)
        g = jnp.where(pick > 0.0, NEG, g)
    return sel


ATTN_GROUP = 4


KM_PAGES = 16
KM_CHUNKS = (PAST_LEN // PAGE_SIZE) // KM_PAGES


def _block_means_step(chunk, page_refs, kmc_ref, blocks):
    blocks_per_step = KM_PAGES // PAGES_PER_BLOCK
    blk = lax.broadcasted_iota(jnp.int32, (1, 1, N_PAST_BLOCKS), 2)
    acc = kmc_ref[0]
    for jj in blocks:
        tot = page_refs[PAGES_PER_BLOCK * jj][...]
        for r in range(1, PAGES_PER_BLOCK):
            tot = tot + page_refs[PAGES_PER_BLOCK * jj + r][...]
        mean = jnp.sum(tot, axis=-1, keepdims=True) * (1.0 / MOBA_BLOCK)
        acc = jnp.where(blk == chunk * blocks_per_step + jj, mean, acc)
    kmc_ref[0] = acc


def _prompt_attn_kernel(pt_ref, q_ref, k_ref, v_ref, *rest):
    del pt_ref
    page_refs = rest[:KM_PAGES]
    o_ref, kmc_ref, kbf_ref, vt_ref, km_ref, sel_ref, s_ref, acc_ref = rest[KM_PAGES:]
    step = (pl.program_id(0) * pl.num_programs(1) + pl.program_id(1)) * pl.num_programs(2) + pl.program_id(2)
    chunk = step % KM_CHUNKS

    @pl.when(chunk == 0)
    def _():
        kmc_ref[...] = jnp.zeros_like(kmc_ref)

    half = KM_PAGES // PAGES_PER_BLOCK // 2
    _prompt_attn_body(q_ref, k_ref, v_ref, o_ref, kbf_ref, vt_ref, km_ref, sel_ref, s_ref, acc_ref,
                      lambda: _block_means_step(chunk, page_refs, kmc_ref, range(0, half)),
                      lambda: _block_means_step(chunk, page_refs, kmc_ref, range(half, 2 * half)))


def _prompt_attn_body(q_ref, k_ref, v_ref, o_ref, kbf_ref, vt_ref, km_ref, sel_ref, s_ref, acc_ref,
                      side_work, side_work_late):
    b = pl.program_id(2)
    nt = (((1,), (1,)), ((), ()))
    lane = lax.broadcasted_iota(jnp.int32, (1, LANES), 1)

    @pl.when(b == 0)
    def _():
        drow = lax.broadcasted_iota(jnp.int32, (LANES, MOBA_BLOCK), 0)
        for j in range(N_PROMPT_BLOCKS):
            kj = k_ref[j * MOBA_BLOCK:(j + 1) * MOBA_BLOCK, :]
            kbf_ref[j] = kj.astype(BF16)
            km_ref[j:j + 1, :] = jnp.mean(kj.astype(F32), axis=0, keepdims=True)
            vt = v_ref[j * MOBA_BLOCK:(j + 1) * MOBA_BLOCK, :].astype(F32).T
            for hh in range(HEADS_PER_TILE):
                vt_ref[hh, j] = jnp.where((drow // HEAD_DIM) == hh, vt, 1.0).astype(BF16)
        blk_f = lax.broadcasted_iota(jnp.int32, (N_PROMPT_BLOCKS, SEQ), 0).astype(F32)
        own_blk = (lax.broadcasted_iota(jnp.int32, (1, SEQ), 1) // MOBA_BLOCK).astype(F32)
        q_all = q_ref[...].astype(F32)
        for hh in range(HEADS_PER_TILE):
            qh = jnp.where((lane // HEAD_DIM) == hh, q_all, 0.0)
            gate_t = lax.dot_general(km_ref[...], qh, nt, precision=lax.Precision.HIGHEST,
                                     preferred_element_type=F32)
            sel = _top_k_rows(gate_t, blk_f, own_blk)
            for qb in range(N_PROMPT_BLOCKS):
                sel_ref[hh, qb] = sel[:, qb * MOBA_BLOCK:(qb + 1) * MOBA_BLOCK]

    side_work()
    q = (q_ref[pl.ds(pl.multiple_of(b * MOBA_BLOCK, MOBA_BLOCK), MOBA_BLOCK), :].astype(F32)
         * (HEAD_DIM ** -0.5 * LOG2_E))
    q_bf = [jnp.where((lane // HEAD_DIM) == hh, q, 0.0).astype(BF16) for hh in range(HEADS_PER_TILE)]
    key_i = lax.broadcasted_iota(jnp.int32, (MOBA_BLOCK, MOBA_BLOCK), 0)
    qry_i = lax.broadcasted_iota(jnp.int32, (MOBA_BLOCK, MOBA_BLOCK), 1)
    n_chunks = lax.shift_right_logical(b + ATTN_GROUP, ATTN_GROUP.bit_length() - 1)

    def score_chunk(c, slot, ms, first):
        ms = list(ms)
        for t in range(ATTN_GROUP):
            own = first and t == 0
            j = b if own else c * ATTN_GROUP + (t - 1)
            kj = kbf_ref[j]
            for hh in range(HEADS_PER_TILE):
                s = lax.dot_general(kj, q_bf[hh], nt, preferred_element_type=F32)
                if own:
                    s = jnp.where(key_i <= qry_i, s, NEG)
                else:
                    s = jnp.where(sel_ref[hh, b, pl.ds(j, 1), :] > 0.0, s, NEG)
                s_ref[hh, slot, t] = s
                ms[hh] = jnp.maximum(ms[hh], jnp.max(s, axis=0, keepdims=True))
        return tuple(ms)

    def value_chunk(c, slot, m_acc, m_new):
        for hh in range(HEADS_PER_TILE):
            acc = jnp.exp2(m_acc[hh] - m_new[hh]) * acc_ref[hh]
            for t in range(ATTN_GROUP):
                j = c * ATTN_GROUP + (t - 1)
                if t == 0:
                    j = jnp.where(c == 0, b, j)
                p = jnp.exp2(s_ref[hh, slot, t] - m_new[hh]).astype(BF16)
                acc = acc + jnp.dot(vt_ref[hh, j], p, preferred_element_type=F32)
            acc_ref[hh] = acc

    acc_ref[...] = jnp.zeros_like(acc_ref)
    neg_row = jnp.full((1, MOBA_BLOCK), NEG, F32)
    m_first = score_chunk(0, 0, (neg_row,) * HEADS_PER_TILE, True)

    def pipelined(c, carry):
        m_acc, m_prev = carry
        value_chunk(c - 1, (c - 1) & 1, m_acc, m_prev)
        m_cur = score_chunk(c, c & 1, m_prev, False)
        return m_prev, m_cur

    m_acc, m_last = lax.fori_loop(1, n_chunks, pipelined, (m_first, m_first))
    side_work_late()
    value_chunk(n_chunks - 1, (n_chunks - 1) & 1, m_acc, m_last)

    drow = lax.broadcasted_iota(jnp.int32, (LANES, MOBA_BLOCK), 0)
    a0 = acc_ref[0]
    a1 = acc_ref[1]
    out_t = jnp.where(drow < HEAD_DIM, a0 / a0[HEAD_DIM:HEAD_DIM + 1, :], a1 / a1[0:1, :])
    o_ref[...] = out_t.T.astype(o_ref.dtype)


def _prompt_attn(z, cache_kt, page_table, layer):
    rows = z.shape[0]
    q0 = 3 * BRANCH_WIDTH // LANES
    k0 = 4 * BRANCH_WIDTH // LANES
    v0 = 5 * BRANCH_WIDTH // LANES
    n_tiles = BRANCH_WIDTH // LANES
    assert HEADS_PER_TILE == 2 and N_PROMPT_BLOCKS % ATTN_GROUP == 0
    assert BATCH * n_tiles * N_PROMPT_BLOCKS == DEC_BATCH * KM_CHUNKS

    def step(n, hp, b):
        return (n * n_tiles + hp) * N_PROMPT_BLOCKS + b

    def page_spec(r):
        def index_map(n, hp, b, pt):
            s = step(n, hp, b)
            return (layer, pt[s // KM_CHUNKS, (s % KM_CHUNKS) * KM_PAGES + r], 0, 0, 0)
        return pl.BlockSpec((None, None, N_HEADS, HEAD_DIM, PAGE_SIZE), index_map)

    return pl.pallas_call(
        _prompt_attn_kernel,
        grid_spec=pltpu.PrefetchScalarGridSpec(
            num_scalar_prefetch=1,
            grid=(BATCH, n_tiles, N_PROMPT_BLOCKS),
            in_specs=[
                pl.BlockSpec((SEQ, LANES), lambda n, hp, b, pt: (n, q0 + hp)),
                pl.BlockSpec((SEQ, LANES), lambda n, hp, b, pt: (n, k0 + hp)),
                pl.BlockSpec((SEQ, LANES), lambda n, hp, b, pt: (n, v0 + hp)),
            ] + [page_spec(r) for r in range(KM_PAGES)],
            out_specs=[
                pl.BlockSpec((MOBA_BLOCK, LANES), lambda n, hp, b, pt: (n * N_PROMPT_BLOCKS + b, hp)),
                pl.BlockSpec((1, N_HEADS, HEAD_DIM, N_PAST_BLOCKS),
                             lambda n, hp, b, pt: (step(n, hp, b) // KM_CHUNKS, 0, 0, 0)),
            ],
            scratch_shapes=[pltpu.VMEM((N_PROMPT_BLOCKS, MOBA_BLOCK, LANES), BF16),
                            pltpu.VMEM((HEADS_PER_TILE, N_PROMPT_BLOCKS, LANES, MOBA_BLOCK), BF16),
                            pltpu.VMEM((N_PROMPT_BLOCKS, LANES), F32),
                            pltpu.VMEM((HEADS_PER_TILE, N_PROMPT_BLOCKS, N_PROMPT_BLOCKS, MOBA_BLOCK), F32),
                            pltpu.VMEM((HEADS_PER_TILE, 2, ATTN_GROUP, MOBA_BLOCK, MOBA_BLOCK), F32),
                            pltpu.VMEM((HEADS_PER_TILE, LANES, MOBA_BLOCK), F32)],
        ),
        out_shape=[jax.ShapeDtypeStruct((rows, BRANCH_WIDTH), BF16),
                   jax.ShapeDtypeStruct((DEC_BATCH, N_HEADS, HEAD_DIM, N_PAST_BLOCKS), F32)],
        compiler_params=_cparams(3),
        name="prompt_attn",
    )(page_table, z, z, z, *([cache_kt] * KM_PAGES))


def _finish_kernel(x_ref, oab_ref, oc_ref, gates_ref, wb_ref, wo_ref, g_ref, o_ref):
    br_a = jnp.dot(oab_ref[:, 0:BRANCH_WIDTH], wb_ref[0], preferred_element_type=F32)
    br_b = jnp.dot(oab_ref[:, BRANCH_WIDTH:2 * BRANCH_WIDTH], wb_ref[1], preferred_element_type=F32)
    br_c = jnp.dot(oc_ref[...], wb_ref[2], preferred_element_type=F32)
    mixed = (gates_ref[:, 0:D_MODEL] * br_a + gates_ref[:, D_MODEL:2 * D_MODEL] * br_b
             + gates_ref[:, 2 * D_MODEL:3 * D_MODEL] * br_c)
    mix = jnp.dot(mixed.astype(BF16), wo_ref[...], preferred_element_type=F32)
    o_ref[...] = x_ref[...] + _rms(mix, g_ref[...])


def _finish(x2d, oab, oc, z, w_branch_bf, w_out_bf, g_post_mix, tm):
    rows = x2d.shape[0]
    return pl.pallas_call(
        _finish_kernel,
        grid=(rows // tm,),
        in_specs=[
            pl.BlockSpec((tm, D_MODEL), lambda i: (i, 0)),
            pl.BlockSpec((tm, 2 * BRANCH_WIDTH), lambda i: (i, 0)),
            pl.BlockSpec((tm, BRANCH_WIDTH), lambda i: (i, 0)),
            pl.BlockSpec((tm, N_BRANCH * D_MODEL), lambda i: (i, 1)),
            pl.BlockSpec((N_BRANCH, BRANCH_WIDTH, D_MODEL), lambda i: (0, 0, 0)),
            pl.BlockSpec((D_MODEL, D_MODEL), lambda i: (0, 0)),
            pl.BlockSpec((1, D_MODEL), lambda i: (0, 0)),
        ],
        out_specs=pl.BlockSpec((tm, D_MODEL), lambda i: (i, 0)),
        out_shape=jax.ShapeDtypeStruct((rows, D_MODEL), F32),
        compiler_params=_cparams(1),
        name="finish",
    )(x2d, oab, oc, z, w_branch_bf, w_out_bf, g_post_mix)


def _ffn_kernel(x_ref, gpre_ref, wg_ref, wu_ref, wo_ref, gpost_ref, o_ref, xn_ref, acc_ref):
    f = pl.program_id(1)

    @pl.when(f == 0)
    def _():
        xn_ref[...] = _rms(x_ref[...], gpre_ref[...]).astype(BF16)
        acc_ref[...] = jnp.zeros_like(acc_ref)

    xn = xn_ref[...]
    gt = jnp.dot(xn, wg_ref[...], preferred_element_type=F32)
    up = jnp.dot(xn, wu_ref[...], preferred_element_type=F32)
    h = (gt * _sigmoid(gt)) * up
    acc_ref[...] += jnp.dot(h.astype(BF16), wo_ref[...], preferred_element_type=F32)

    @pl.when(f == pl.num_programs(1) - 1)
    def _():
        o_ref[...] = x_ref[...] + _rms(acc_ref[...], gpost_ref[...])


def _ffn(x2d, g_pre, w_ff_in_bf, w_ff_out_bf, g_post, tm):
    rows = x2d.shape[0]
    n_f = D_FF // FF_TILE
    return pl.pallas_call(
        _ffn_kernel,
        grid=(rows // tm, n_f),
        in_specs=[
            pl.BlockSpec((tm, D_MODEL), lambda i, f: (i, 0)),
            pl.BlockSpec((1, D_MODEL), lambda i, f: (0, 0)),
            pl.BlockSpec((D_MODEL, FF_TILE), lambda i, f: (0, f)),
            pl.BlockSpec((D_MODEL, FF_TILE), lambda i, f: (0, n_f + f)),
            pl.BlockSpec((FF_TILE, D_MODEL), lambda i, f: (f, 0)),
            pl.BlockSpec((1, D_MODEL), lambda i, f: (0, 0)),
        ],
        out_specs=pl.BlockSpec((tm, D_MODEL), lambda i, f: (i, 0)),
        out_shape=jax.ShapeDtypeStruct((rows, D_MODEL), F32),
        scratch_shapes=[pltpu.VMEM((tm, D_MODEL), BF16), pltpu.VMEM((tm, D_MODEL), F32)],
        compiler_params=_cparams(2),
        name="ffn",
    )(x2d, g_pre, w_ff_in_bf, w_ff_in_bf, w_ff_out_bf, g_post)


SAMPLE_ROWS = DEC_BATCH * DEC_SEQ


def _sample_mix_kernel(u_ref, vn_ref, p_ref, st_ref, wv_ref, bv_ref, wp_ref, sc_ref, oab_ref):
    def slab(ref, s):
        return ref[s * DEC_BATCH:(s + 1) * DEC_BATCH, :]

    for t in range(DEC_SEQ):
        s_t = bv_ref[t:t + 1, :]
        for sp in range(t + 1):
            s_t = s_t + wv_ref[t, sp:sp + 1, :] * slab(vn_ref, sp)
        oab_ref[t * DEC_BATCH:(t + 1) * DEC_BATCH, 0:BRANCH_WIDTH] = (slab(u_ref, t) * s_t).astype(oab_ref.dtype)

    ext = [st_ref[r] for r in range(POOL_KEEP)] + [slab(p_ref, s) for s in range(DEC_SEQ)]
    lane = lax.broadcasted_iota(jnp.int32, (1, BRANCH_WIDTH), 1)
    d_rows = []
    for s in range(DEC_SEQ):
        end = POOL_KEEP + s + 1
        mean = jnp.zeros((DEC_BATCH, BRANCH_WIDTH), F32)
        run = jnp.zeros((DEC_BATCH, BRANCH_WIDTH), F32)
        taken = 0
        for gi, win in enumerate(POOL_WINDOWS):
            for r in range(end - win, end - taken):
                run = run + ext[r]
            taken = win
            mean = jnp.where((lane // POOL_GROUP_WIDTH) == gi, run / float(win), mean)
        d_rows.append(mean - ext[POOL_KEEP + s])
    d = jnp.concatenate(d_rows, axis=0).astype(BF16)
    for gi in range(len(POOL_WINDOWS)):
        cs = slice(gi * POOL_GROUP_WIDTH, (gi + 1) * POOL_GROUP_WIDTH)
        y = jnp.dot(d[:, cs], wp_ref[gi], preferred_element_type=F32)
        oab_ref[:, BRANCH_WIDTH + gi * POOL_GROUP_WIDTH:BRANCH_WIDTH + (gi + 1) * POOL_GROUP_WIDTH] = (
            y * sc_ref[:, cs]).astype(oab_ref.dtype)


def _sample_mix(z, state_t, w_vec, b_vec, w_pool_bf, pool_scale):
    return pl.pallas_call(
        _sample_mix_kernel,
        grid=(1,),
        in_specs=[
            pl.BlockSpec((SAMPLE_ROWS, BRANCH_WIDTH), lambda i: (0, 0)),
            pl.BlockSpec((SAMPLE_ROWS, BRANCH_WIDTH), lambda i: (0, 1)),
            pl.BlockSpec((SAMPLE_ROWS, BRANCH_WIDTH), lambda i: (0, 2)),
            pl.BlockSpec((POOL_KEEP, DEC_BATCH, BRANCH_WIDTH), lambda i: (0, 0, 0)),
            pl.BlockSpec((DEC_SEQ, DEC_SEQ, BRANCH_WIDTH), lambda i: (0, 0, 0)),
            pl.BlockSpec((DEC_SEQ, BRANCH_WIDTH), lambda i: (0, 0)),
            pl.BlockSpec((len(POOL_WINDOWS), POOL_GROUP_WIDTH, POOL_GROUP_WIDTH), lambda i: (0, 0, 0)),
            pl.BlockSpec((1, BRANCH_WIDTH), lambda i: (0, 0)),
        ],
        out_specs=pl.BlockSpec((SAMPLE_ROWS, 2 * BRANCH_WIDTH), lambda i: (0, 0)),
        out_shape=jax.ShapeDtypeStruct((SAMPLE_ROWS, 2 * BRANCH_WIDTH), BF16),
        compiler_params=_cparams(1),
        name="sample_mix",
    )(z, z, z, state_t, w_vec, b_vec, w_pool_bf, pool_scale)


TOPK_SEQS = 8


def _gate_topk_kernel(q_ref, km_ref, idx_ref):
    lane_f = lax.broadcasted_iota(jnp.int32, (DEC_SEQ, N_PAST_BLOCKS), 1).astype(F32)
    for n in range(TOPK_SEQS):
        for h in range(N_HEADS):
            g = jnp.dot(q_ref[n, h], km_ref[n, h], precision=lax.Precision.HIGHEST,
                        preferred_element_type=F32)
            for r in range(MOBA_TOP_K):
                mx = jnp.max(g, axis=-1, keepdims=True)
                first = jnp.min(jnp.where(g == mx, lane_f, float(N_PAST_BLOCKS)), axis=-1, keepdims=True)
                idx_ref[n, h, :, r:r + 1] = first.astype(jnp.int32)
                g = jnp.where(lane_f == first, NEG, g)


def _gate_topk(q_nh, km_nh):
    return pl.pallas_call(
        _gate_topk_kernel,
        grid=(DEC_BATCH // TOPK_SEQS,),
        in_specs=[
            pl.BlockSpec((TOPK_SEQS, N_HEADS, DEC_SEQ, HEAD_DIM), lambda n: (n, 0, 0, 0)),
            pl.BlockSpec((TOPK_SEQS, N_HEADS, HEAD_DIM, N_PAST_BLOCKS), lambda n: (n, 0, 0, 0)),
        ],
        out_specs=pl.BlockSpec((TOPK_SEQS, N_HEADS, DEC_SEQ, MOBA_TOP_K), lambda n: (n, 0, 0, 0)),
        out_shape=jax.ShapeDtypeStruct((DEC_BATCH, N_HEADS, DEC_SEQ, MOBA_TOP_K), jnp.int32),
        compiler_params=_cparams(1),
        name="gate_topk",
    )(q_nh, km_nh)


N_SEL_PAGES = MOBA_TOP_K * PAGES_PER_BLOCK


N_SLABS = DEC_SEQ * N_SEL_PAGES


def _sample_attn_kernel(pt_ref, idx_ref, q_ref, kn_ref, vn_ref, ck_ref, cv_ref, o_ref,
                        kbuf, vbuf, sem, *, layer):
    n = pl.program_id(0)
    slot = n & 1

    def slab_copies(seq, buf_slot, h, i):
        s, r = divmod(i, N_SEL_PAGES)
        blk = idx_ref[((seq * N_HEADS + h) * DEC_SEQ + s) * MOBA_TOP_K + r // PAGES_PER_BLOCK]
        page = pt_ref[seq, blk * PAGES_PER_BLOCK + r % PAGES_PER_BLOCK]
        return (pltpu.make_async_copy(ck_ref.at[layer, page, h], kbuf.at[buf_slot, h, i], sem.at[buf_slot]),
                pltpu.make_async_copy(cv_ref.at[layer, page, h], vbuf.at[buf_slot, h, i], sem.at[buf_slot]))

    def for_each_slab(seq, buf_slot, fn):
        def per_head(h, carry):
            for i in range(N_SLABS):
                for cp in slab_copies(seq, buf_slot, h, i):
                    fn(cp)
            return carry
        lax.fori_loop(0, N_HEADS, per_head, 0)

    @pl.when(n == 0)
    def _():
        for_each_slab(n, slot, lambda cp: cp.start())

    @pl.when(n + 1 < pl.num_programs(0))
    def _():
        for_each_slab(n + 1, 1 - slot, lambda cp: cp.start())

    for_each_slab(n, slot, lambda cp: cp.wait())

    scale = HEAD_DIM ** -0.5
    pos = lax.broadcasted_iota(jnp.int32, (1, DEC_SEQ), 1)

    def per_head(h, carry):
        k_new = kn_ref[0, h]
        v_new = vn_ref[0, h]
        for s in range(DEC_SEQ):
            q = q_ref[0, h, :, s:s + 1] * scale
            s_sel = [jnp.sum(q * kbuf[slot, h, s * N_SEL_PAGES + r], axis=0, keepdims=True)
                     for r in range(N_SEL_PAGES)]
            s_own = jnp.where(pos <= s, jnp.sum(q * k_new, axis=0, keepdims=True), NEG)
            m_sel = functools.reduce(jnp.maximum, s_sel)
            m = jnp.maximum(jnp.max(m_sel, axis=-1, keepdims=True), jnp.max(s_own, axis=-1, keepdims=True))
            p_sel = [jnp.exp(sr - m) for sr in s_sel]
            p_own = jnp.exp(s_own - m)
            l = (jnp.sum(functools.reduce(jnp.add, p_sel), axis=-1, keepdims=True)
                 + jnp.sum(p_own, axis=-1, keepdims=True))
            pv = functools.reduce(jnp.add, [vbuf[slot, h, s * N_SEL_PAGES + r] * p_sel[r]
                                            for r in range(N_SEL_PAGES)])
            o = jnp.sum(pv, axis=-1, keepdims=True) + jnp.sum(v_new * p_own, axis=-1, keepdims=True)
            o_ref[0, h, :, s:s + 1] = o / l
        return carry

    lax.fori_loop(0, N_HEADS, per_head, 0)


def _sample_attn(q_cols, k_cols, v_cols, cache_kt, cache_vt, page_table, idx, layer):
    def new_spec():
        return pl.BlockSpec((1, N_HEADS, HEAD_DIM, DEC_SEQ), lambda n, pt, ix: (n, 0, 0, 0))

    buf = pltpu.VMEM((2, N_HEADS, N_SLABS, HEAD_DIM, PAGE_SIZE), F32)
    return pl.pallas_call(
        functools.partial(_sample_attn_kernel, layer=layer),
        grid_spec=pltpu.PrefetchScalarGridSpec(
            num_scalar_prefetch=2,
            grid=(DEC_BATCH,),
            in_specs=[new_spec(), new_spec(), new_spec(),
                      pl.BlockSpec(memory_space=pl.ANY), pl.BlockSpec(memory_space=pl.ANY)],
            out_specs=new_spec(),
            scratch_shapes=[buf, buf, pltpu.SemaphoreType.DMA((2,))],
        ),
        out_shape=jax.ShapeDtypeStruct((DEC_BATCH, N_HEADS, HEAD_DIM, DEC_SEQ), F32),
        compiler_params=_cparams(1),
        name="sample_attn",
    )(page_table, idx.reshape(-1), q_cols, k_cols, v_cols, cache_kt, cache_vt)


def _rope_tables(pos):
    half = HEAD_DIM // 2
    inv = ROPE_THETA ** (-jnp.arange(half, dtype=F32) / half)
    ang = pos.astype(F32)[:, None] * inv[None, :]
    cos, sin = jnp.cos(ang), jnp.sin(ang)
    reps = LANES // half
    cos_t = jnp.tile(cos, (1, reps))
    sin_t = jnp.concatenate([-sin, sin] * (reps // 2), axis=1)
    return cos_t, sin_t


def _pages_out(per_layer):
    t = jnp.stack(per_layer).reshape(DEPTH, BATCH, SEQ // PAGE_SIZE, N_HEADS, HEAD_DIM, PAGE_SIZE)
    return t.transpose(0, 1, 2, 3, 5, 4)


def _sample_heads(a2d):
    return a2d.reshape(DEC_SEQ, DEC_BATCH, N_HEADS, HEAD_DIM).transpose(1, 2, 0, 3)


def _sample_rows(a2d):
    return a2d.reshape(DEC_SEQ, DEC_BATCH, a2d.shape[-1]).transpose(1, 0, 2)


def kernel(x_prompt, x_sample, cache_k, cache_v, state_pool, page_table, w_in, g_v, w_s, b_s, w_pool, pool_scale, w_branch, w_out, g_pre_mix, g_post_mix, g_pre_ffn, g_post_ffn, w_ff_in, w_ff_out):
    cos_p, sin_p = _rope_tables(jnp.arange(SEQ))
    cos_s, sin_s = _rope_tables(PAST_LEN + jnp.arange(SAMPLE_ROWS) // DEC_BATCH)

    xp = x_prompt.reshape(BATCH * SEQ, D_MODEL)
    xs = x_sample.transpose(1, 0, 2).reshape(SAMPLE_ROWS, D_MODEL)
    page_table = page_table.astype(jnp.int32)
    cache_kt = cache_k.transpose(0, 1, 2, 4, 3)
    cache_vt = cache_v.transpose(0, 1, 2, 4, 3)

    kp_l, vp_l, poolp_l, ks_l, vs_l, pools_l, cvs_l = [], [], [], [], [], [], []
    for l in range(DEPTH):
        w_in_bf = w_in[l].astype(BF16)
        w_pool_bf = w_pool[l].astype(BF16)
        w_branch_bf = w_branch[l].astype(BF16)
        w_out_bf = w_out[l].astype(BF16)
        w_ff_in_bf = w_ff_in[l].astype(BF16)
        w_ff_out_bf = w_ff_out[l].astype(BF16)
        row = lambda a: a[l].reshape(1, -1)
        b_s_t = b_s[l].T

        zp, kt_pages, vt_pages = _inproj(xp, row(g_pre_mix), w_in_bf, row(g_v), cos_p, sin_p,
                                         tm=1024, emit_pages=True, z_dtype=BF16)
        oab = _prompt_mix(zp, w_s[l], b_s_t, w_pool_bf, row(pool_scale))
        oc, km = _prompt_attn(zp, cache_kt, page_table, l)
        x1 = _finish(xp, oab, oc, zp, w_branch_bf, w_out_bf, row(g_post_mix), tm=256)
        xp = _ffn(x1, row(g_pre_ffn), w_ff_in_bf, w_ff_out_bf, row(g_post_ffn), tm=512)
        kp_l.append(kt_pages)
        vp_l.append(vt_pages)
        poolp_l.append(zp[:, 2 * BRANCH_WIDTH:3 * BRANCH_WIDTH]
                       .reshape(BATCH, SEQ, BRANCH_WIDTH)[:, SEQ - POOL_KEEP:].astype(F32))

        zs = _inproj(xs, row(g_pre_mix), w_in_bf, row(g_v), cos_s, sin_s, tm=SAMPLE_ROWS,
                     emit_pages=False, z_dtype=F32)
        w_vec = jnp.repeat(w_s[l][:, :DEC_SEQ, :DEC_SEQ].transpose(1, 2, 0), CM_GROUP_WIDTH, axis=-1)
        b_vec = jnp.repeat(b_s[l][:, :DEC_SEQ].T, CM_GROUP_WIDTH, axis=-1)
        state_t = state_pool[l].transpose(1, 0, 2)
        oab_s = _sample_mix(zs, state_t, w_vec, b_vec, w_pool_bf, row(pool_scale))
        q_nh = _sample_heads(zs[:, 3 * BRANCH_WIDTH:4 * BRANCH_WIDTH])
        k_nh = _sample_heads(zs[:, 4 * BRANCH_WIDTH:5 * BRANCH_WIDTH])
        v_nh = _sample_heads(zs[:, 5 * BRANCH_WIDTH:6 * BRANCH_WIDTH])
        idx = _gate_topk(q_nh, km)
        to_cols = lambda a: a.transpose(0, 1, 3, 2)
        oc_cols = _sample_attn(to_cols(q_nh), to_cols(k_nh), to_cols(v_nh), cache_kt, cache_vt,
                               page_table, idx, l)
        oc_s = oc_cols.transpose(3, 0, 1, 2).reshape(SAMPLE_ROWS, BRANCH_WIDTH).astype(BF16)
        x1s = _finish(xs, oab_s, oc_s, zs, w_branch_bf, w_out_bf, row(g_post_mix), tm=SAMPLE_ROWS)
        xs = _ffn(x1s, row(g_pre_ffn), w_ff_in_bf, w_ff_out_bf, row(g_post_ffn), tm=SAMPLE_ROWS)
        ks_l.append(k_nh)
        vs_l.append(v_nh)
        p_s = _sample_rows(zs[:, 2 * BRANCH_WIDTH:3 * BRANCH_WIDTH])
        pools_l.append(jnp.concatenate([state_pool[l][:, DEC_SEQ:], p_s], axis=1))
        cvs_l.append(_sample_rows(zs[:, BRANCH_WIDTH:2 * BRANCH_WIDTH]))

    y_prompt = xp.reshape(BATCH, SEQ, D_MODEL)
    y_sample = _sample_rows(xs)
    return (y_prompt, y_sample, _pages_out(kp_l), _pages_out(vp_l), jnp.stack(poolp_l),
            jnp.stack(ks_l), jnp.stack(vs_l), jnp.stack(pools_l), jnp.stack(cvs_l))
```

```python
import functools

import numpy as np
import jax
import jax.numpy as jnp
from jax import lax
from jax.experimental import pallas as pl
from jax.experimental.pallas import tpu as pltpu

D_MODEL = 1024
BATCH = 4
SEQ = 4096
DEPTH = 2
DEC_BATCH = 32
DEC_SEQ = 4
PAST_LEN = 16384
PAGE_SIZE = 128

BRANCH_WIDTH = D_MODEL // 2
HEAD_DIM = 64
N_HEADS = BRANCH_WIDTH // HEAD_DIM
MOBA_BLOCK = 256
MOBA_TOP_K = 3
ROPE_THETA = 10000.0
CM_CHUNK = 128
CM_GROUPS = 8
CM_GROUP_WIDTH = BRANCH_WIDTH // CM_GROUPS
POOL_WINDOWS = (2, 4, 8, 16)
POOL_GROUP_WIDTH = BRANCH_WIDTH // len(POOL_WINDOWS)
POOL_KEEP = max(POOL_WINDOWS) - 1
N_BRANCH = 3
D_FF = -(-8 * D_MODEL // (3 * 256)) * 256
IN_WIDTH = 6 * BRANCH_WIDTH + N_BRANCH * D_MODEL
NORM_EPS = 1e-6
NEG = -1e30
LOG2_E = 1.4426950408889634

LANES = 128
COL_TILE = BRANCH_WIDTH
N_COL_TILES = IN_WIDTH // COL_TILE
GATE_COL0 = 6 * BRANCH_WIDTH
PAGES_PER_BLOCK = MOBA_BLOCK // PAGE_SIZE
N_PAST_BLOCKS = PAST_LEN // MOBA_BLOCK
N_PROMPT_BLOCKS = SEQ // MOBA_BLOCK
FF_TILE = D_FF // 2
VMEM_LIMIT = 56 * 1024 * 1024

F32 = jnp.float32
BF16 = jnp.bfloat16


def _cparams(n_axes):
    return pltpu.CompilerParams(dimension_semantics=("arbitrary",) * n_axes,
                                vmem_limit_bytes=VMEM_LIMIT)


def _rms(x, g):
    return (x * lax.rsqrt(jnp.mean(x * x, axis=-1, keepdims=True) + NORM_EPS)) * g


def _sigmoid(x):
    return 1.0 / (1.0 + jnp.exp(-x))


def _inproj_kernel(x_ref, g_ref, w_ref, gv_ref, cos_ref, sin_ref, z_ref, *rest, row_chunk, emit_pages):
    xn_ref = rest[-1]
    kt_ref, vt_ref = rest[:2] if emit_pages else (None, None)
    j = pl.program_id(1)
    tm = x_ref.shape[0]
    pages_per_chunk = row_chunk // PAGE_SIZE

    @pl.when(j == 0)
    def _():
        xn_ref[...] = _rms(x_ref[...], g_ref[...]).astype(BF16)

    def tile(epilogue, t_ref=None):
        for r in range(tm // row_chunk):
            rs = slice(r * row_chunk, (r + 1) * row_chunk)
            acc = jnp.dot(xn_ref[rs, :], w_ref[...], preferred_element_type=F32)
            epilogue(acc, rs, r, t_ref)

    def store_pages(t_ref, vals, r, c):
        for pg in range(pages_per_chunk):
            t_ref[r * pages_per_chunk + pg, c * LANES:(c + 1) * LANES, :] = (
                vals[pg * PAGE_SIZE:(pg + 1) * PAGE_SIZE, :].T)

    def gelu_out(acc, rs, r, t_ref):
        z_ref[rs, :] = jax.nn.gelu(acc).astype(z_ref.dtype)

    def gelu_layernorm_out(acc, rs, r, t_ref):
        v = jax.nn.gelu(acc)
        vc = v - jnp.mean(v, axis=-1, keepdims=True)
        y = vc * lax.rsqrt(jnp.mean(vc * vc, axis=-1, keepdims=True) + NORM_EPS)
        z_ref[rs, :] = (y * gv_ref[...]).astype(z_ref.dtype)

    def raw_out(acc, rs, r, t_ref):
        z_ref[rs, :] = acc.astype(z_ref.dtype)
        if t_ref is not None:
            for c in range(COL_TILE // LANES):
                store_pages(t_ref, acc[:, c * LANES:(c + 1) * LANES], r, c)

    def rope_out(acc, rs, r, t_ref):
        lane = lax.broadcasted_iota(jnp.int32, (1, LANES), 1)
        first_half = (lane % HEAD_DIM) < (HEAD_DIM // 2)
        cos = cos_ref[rs, :]
        sin = sin_ref[rs, :]
        for c in range(COL_TILE // LANES):
            a = acc[:, c * LANES:(c + 1) * LANES]
            partner = jnp.where(first_half,
                                pltpu.roll(a, LANES - HEAD_DIM // 2, axis=1),
                                pltpu.roll(a, HEAD_DIM // 2, axis=1))
            rot = a * cos + partner * sin
            z_ref[rs, c * LANES:(c + 1) * LANES] = rot.astype(z_ref.dtype)
            if t_ref is not None:
                store_pages(t_ref, rot, r, c)

    def sigmoid_out(acc, rs, r, t_ref):
        z_ref[rs, :] = _sigmoid(acc).astype(z_ref.dtype)

    pl.when(j == 0)(lambda: tile(gelu_out))
    pl.when(j == 1)(lambda: tile(gelu_layernorm_out))
    pl.when(j == 2)(lambda: tile(raw_out))
    pl.when(j == 3)(lambda: tile(rope_out))
    pl.when(j == 4)(lambda: tile(rope_out, kt_ref))
    pl.when(j == 5)(lambda: tile(raw_out, vt_ref))
    pl.when(j >= GATE_COL0 // COL_TILE)(lambda: tile(sigmoid_out))


def _inproj(x2d, g_pre, w_in_bf, g_v, cos_t, sin_t, tm, emit_pages, z_dtype):
    rows = x2d.shape[0]
    t_tiles = cos_t.shape[0] // tm
    z_spec = pl.BlockSpec((tm, COL_TILE), lambda i, j: (i, j))
    z_shape = jax.ShapeDtypeStruct((rows, IN_WIDTH), z_dtype)
    if emit_pages:
        page_spec = pl.BlockSpec((tm // PAGE_SIZE, BRANCH_WIDTH, PAGE_SIZE), lambda i, j: (i, 0, 0))
        page_shape = jax.ShapeDtypeStruct((rows // PAGE_SIZE, BRANCH_WIDTH, PAGE_SIZE), F32)
        out_specs, out_shape = [z_spec, page_spec, page_spec], [z_shape, page_shape, page_shape]
    else:
        out_specs, out_shape = z_spec, z_shape
    return pl.pallas_call(
        functools.partial(_inproj_kernel, row_chunk=min(tm, 256), emit_pages=emit_pages),
        grid=(rows // tm, N_COL_TILES),
        in_specs=[
            pl.BlockSpec((tm, D_MODEL), lambda i, j: (i, 0)),
            pl.BlockSpec((1, D_MODEL), lambda i, j: (0, 0)),
            pl.BlockSpec((D_MODEL, COL_TILE), lambda i, j: (0, j)),
            pl.BlockSpec((1, COL_TILE), lambda i, j: (0, 0)),
            pl.BlockSpec((tm, LANES), lambda i, j: (i % t_tiles, 0)),
            pl.BlockSpec((tm, LANES), lambda i, j: (i % t_tiles, 0)),
        ],
        out_specs=out_specs,
        out_shape=out_shape,
        scratch_shapes=[pltpu.VMEM((tm, D_MODEL), BF16)],
        compiler_params=_cparams(2),
        name="inproj",
    )(x2d, g_pre, w_in_bf, g_v, cos_t, sin_t)


MIX_TILE = 2 * CM_CHUNK
HALO = 16


def _prompt_mix_kernel(u_ref, vn_ref, p_ref, prev_ref, ws_ref, bs_ref, wp_ref, sc_ref,
                       oab_ref, ext_ref):
    i = pl.program_id(0)
    tiles_per_seq = SEQ // MIX_TILE
    first = (i % tiles_per_seq) == 0

    row = lax.broadcasted_iota(jnp.int32, (CM_CHUNK, CM_CHUNK), 0)
    col = lax.broadcasted_iota(jnp.int32, (CM_CHUNK, CM_CHUNK), 1)
    tri = row >= col
    lane = lax.broadcasted_iota(jnp.int32, (1, LANES), 1)
    lo_half = lane < CM_GROUP_WIDTH
    w_tri = [jnp.where(tri, ws_ref[g], 0.0).astype(BF16) for g in range(CM_GROUPS)]
    for c in range(MIX_TILE // CM_CHUNK):
        rs = slice(c * CM_CHUNK, (c + 1) * CM_CHUNK)
        for pr in range(CM_GROUPS // 2):
            cs = slice(pr * LANES, (pr + 1) * LANES)
            vn = vn_ref[rs, cs].astype(F32)
            v_lo = jnp.where(lo_half, vn, 0.0).astype(BF16)
            v_hi = jnp.where(lo_half, 0.0, vn).astype(BF16)
            s = (jnp.dot(w_tri[2 * pr], v_lo, preferred_element_type=F32)
                 + jnp.dot(w_tri[2 * pr + 1], v_hi, preferred_element_type=F32))
            bias = jnp.where(lo_half, bs_ref[:, 2 * pr:2 * pr + 1], bs_ref[:, 2 * pr + 1:2 * pr + 2])
            oab_ref[rs, cs] = (u_ref[rs, cs].astype(F32) * (s + bias)).astype(oab_ref.dtype)

    ext_ref[0:HALO, :] = jnp.where(first, 0.0, prev_ref[...].astype(F32))
    ext_ref[HALO:HALO + MIX_TILE, :] = p_ref[...].astype(F32)
    t_in_seq = (i % tiles_per_seq) * MIX_TILE + lax.broadcasted_iota(jnp.int32, (MIX_TILE, 1), 0)
    for gi, win in enumerate(POOL_WINDOWS):
        cs = slice(gi * POOL_GROUP_WIDTH, (gi + 1) * POOL_GROUP_WIDTH)
        tot = ext_ref[HALO:HALO + MIX_TILE, cs]
        for k in range(1, win):
            tot = tot + ext_ref[HALO - k:HALO - k + MIX_TILE, cs]
        cnt = jnp.minimum(t_in_seq + 1, win).astype(F32)
        d = tot / cnt - ext_ref[HALO:HALO + MIX_TILE, cs]
        y = jnp.dot(d.astype(BF16), wp_ref[gi], preferred_element_type=F32)
        oab_ref[:, BRANCH_WIDTH + gi * POOL_GROUP_WIDTH:BRANCH_WIDTH + (gi + 1) * POOL_GROUP_WIDTH] = (
            y * sc_ref[:, cs]).astype(oab_ref.dtype)


def _prompt_mix(z, w_s, b_s_t, w_pool_bf, pool_scale):
    rows = z.shape[0]
    halo_blocks = MIX_TILE // HALO
    return pl.pallas_call(
        _prompt_mix_kernel,
        grid=(rows // MIX_TILE,),
        in_specs=[
            pl.BlockSpec((MIX_TILE, BRANCH_WIDTH), lambda i: (i, 0)),
            pl.BlockSpec((MIX_TILE, BRANCH_WIDTH), lambda i: (i, 1)),
            pl.BlockSpec((MIX_TILE, BRANCH_WIDTH), lambda i: (i, 2)),
            pl.BlockSpec((HALO, BRANCH_WIDTH), lambda i: (jnp.maximum(i * halo_blocks - 1, 0), 2)),
            pl.BlockSpec((CM_GROUPS, CM_CHUNK, CM_CHUNK), lambda i: (0, 0, 0)),
            pl.BlockSpec((CM_CHUNK, CM_GROUPS), lambda i: (0, 0)),
            pl.BlockSpec((len(POOL_WINDOWS), POOL_GROUP_WIDTH, POOL_GROUP_WIDTH), lambda i: (0, 0, 0)),
            pl.BlockSpec((1, BRANCH_WIDTH), lambda i: (0, 0)),
        ],
        out_specs=pl.BlockSpec((MIX_TILE, 2 * BRANCH_WIDTH), lambda i: (i, 0)),
        out_shape=jax.ShapeDtypeStruct((rows, 2 * BRANCH_WIDTH), BF16),
        scratch_shapes=[pltpu.VMEM((HALO + MIX_TILE, BRANCH_WIDTH), F32)],
        compiler_params=_cparams(1),
        name="prompt_mix",
    )(z, z, z, z, w_s, b_s_t, w_pool_bf, pool_scale)


HEADS_PER_TILE = LANES // HEAD_DIM


def _top_k_rows(gate_t, blk_f, n_valid):
    n_blk = gate_t.shape[0]
    g = jnp.where(blk_f < n_valid, gate_t, NEG)
    sel = jnp.zeros(gate_t.shape, F32)
    for r in range(MOBA_TOP_K):
        mx = jnp.max(g, axis=0, keepdims=True)
        first = jnp.min(jnp.where(g == mx, blk_f, float(n_blk)), axis=0, keepdims=True)
        pick = jnp.where(blk_f == first, jnp.where(n_valid > float(r), 1.0, 0.0), 0.0)
        sel = jnp.maximum(sel, pick)
        g = jnp.where(pick > 0.0, NEG, g)
    return sel


ATTN_GROUP = 4


KM_PAGES = 16
KM_CHUNKS = (PAST_LEN // PAGE_SIZE) // KM_PAGES


def _block_means_step(chunk, page_refs, kmc_ref, blocks):
    blocks_per_step = KM_PAGES // PAGES_PER_BLOCK
    blk = lax.broadcasted_iota(jnp.int32, (1, 1, N_PAST_BLOCKS), 2)
    acc = kmc_ref[0]
    for jj in blocks:
        tot = page_refs[PAGES_PER_BLOCK * jj][...]
        for r in range(1, PAGES_PER_BLOCK):
            tot = tot + page_refs[PAGES_PER_BLOCK * jj + r][...]
        mean = jnp.sum(tot, axis=-1, keepdims=True) * (1.0 / MOBA_BLOCK)
        acc = jnp.where(blk == chunk * blocks_per_step + jj, mean, acc)
    kmc_ref[0] = acc


def _prompt_attn_kernel(pt_ref, q_ref, k_ref, v_ref, *rest):
    del pt_ref
    page_refs = rest[:KM_PAGES]
    o_ref, kmc_ref, kbf_ref, vt_ref, km_ref, sel_ref, s_ref, acc_ref = rest[KM_PAGES:]
    step = (pl.program_id(0) * pl.num_programs(1) + pl.program_id(1)) * pl.num_programs(2) + pl.program_id(2)
    chunk = step % KM_CHUNKS

    @pl.when(chunk == 0)
    def _():
        kmc_ref[...] = jnp.zeros_like(kmc_ref)

    half = KM_PAGES // PAGES_PER_BLOCK // 2
    _prompt_attn_body(q_ref, k_ref, v_ref, o_ref, kbf_ref, vt_ref, km_ref, sel_ref, s_ref, acc_ref,
                      lambda: _block_means_step(chunk, page_refs, kmc_ref, range(0, half)),
                      lambda: _block_means_step(chunk, page_refs, kmc_ref, range(half, 2 * half)))


def _prompt_attn_body(q_ref, k_ref, v_ref, o_ref, kbf_ref, vt_ref, km_ref, sel_ref, s_ref, acc_ref,
                      side_work, side_work_late):
    b = pl.program_id(2)
    nt = (((1,), (1,)), ((), ()))
    lane = lax.broadcasted_iota(jnp.int32, (1, LANES), 1)

    @pl.when(b == 0)
    def _():
        drow = lax.broadcasted_iota(jnp.int32, (LANES, MOBA_BLOCK), 0)
        for j in range(N_PROMPT_BLOCKS):
            kj = k_ref[j * MOBA_BLOCK:(j + 1) * MOBA_BLOCK, :]
            kbf_ref[j] = kj.astype(BF16)
            km_ref[j:j + 1, :] = jnp.mean(kj.astype(F32), axis=0, keepdims=True)
            vt = v_ref[j * MOBA_BLOCK:(j + 1) * MOBA_BLOCK, :].astype(F32).T
            for hh in range(HEADS_PER_TILE):
                vt_ref[hh, j] = jnp.where((drow // HEAD_DIM) == hh, vt, 1.0).astype(BF16)
        blk_f = lax.broadcasted_iota(jnp.int32, (N_PROMPT_BLOCKS, SEQ), 0).astype(F32)
        own_blk = (lax.broadcasted_iota(jnp.int32, (1, SEQ), 1) // MOBA_BLOCK).astype(F32)
        q_all = q_ref[...].astype(F32)
        for hh in range(HEADS_PER_TILE):
            qh = jnp.where((lane // HEAD_DIM) == hh, q_all, 0.0)
            gate_t = lax.dot_general(km_ref[...], qh, nt, precision=lax.Precision.HIGHEST,
                                     preferred_element_type=F32)
            sel = _top_k_rows(gate_t, blk_f, own_blk)
            for qb in range(N_PROMPT_BLOCKS):
                sel_ref[hh, qb] = sel[:, qb * MOBA_BLOCK:(qb + 1) * MOBA_BLOCK]

    side_work()
    q = (q_ref[pl.ds(pl.multiple_of(b * MOBA_BLOCK, MOBA_BLOCK), MOBA_BLOCK), :].astype(F32)
         * (HEAD_DIM ** -0.5 * LOG2_E))
    q_bf = [jnp.where((lane // HEAD_DIM) == hh, q, 0.0).astype(BF16) for hh in range(HEADS_PER_TILE)]
    key_i = lax.broadcasted_iota(jnp.int32, (MOBA_BLOCK, MOBA_BLOCK), 0)
    qry_i = lax.broadcasted_iota(jnp.int32, (MOBA_BLOCK, MOBA_BLOCK), 1)
    n_chunks = lax.shift_right_logical(b + ATTN_GROUP, ATTN_GROUP.bit_length() - 1)

    def score_chunk(c, slot, ms, first):
        ms = list(ms)
        for t in range(ATTN_GROUP):
            own = first and t == 0
            j = b if own else c * ATTN_GROUP + (t - 1)
            kj = kbf_ref[j]
            for hh in range(HEADS_PER_TILE):
                s = lax.dot_general(kj, q_bf[hh], nt, preferred_element_type=F32)
                if own:
                    s = jnp.where(key_i <= qry_i, s, NEG)
                else:
                    s = jnp.where(sel_ref[hh, b, pl.ds(j, 1), :] > 0.0, s, NEG)
                s_ref[hh, slot, t] = s
                ms[hh] = jnp.maximum(ms[hh], jnp.max(s, axis=0, keepdims=True))
        return tuple(ms)

    def value_chunk(c, slot, m_acc, m_new):
        for hh in range(HEADS_PER_TILE):
            acc = jnp.exp2(m_acc[hh] - m_new[hh]) * acc_ref[hh]
            for t in range(ATTN_GROUP):
                j = c * ATTN_GROUP + (t - 1)
                if t == 0:
                    j = jnp.where(c == 0, b, j)
                p = jnp.exp2(s_ref[hh, slot, t] - m_new[hh]).astype(BF16)
                acc = acc + jnp.dot(vt_ref[hh, j], p, preferred_element_type=F32)
            acc_ref[hh] = acc

    acc_ref[...] = jnp.zeros_like(acc_ref)
    neg_row = jnp.full((1, MOBA_BLOCK), NEG, F32)
    m_first = score_chunk(0, 0, (neg_row,) * HEADS_PER_TILE, True)

    def pipelined(c, carry):
        m_acc, m_prev = carry
        value_chunk(c - 1, (c - 1) & 1, m_acc, m_prev)
        m_cur = score_chunk(c, c & 1, m_prev, False)
        return m_prev, m_cur

    m_acc, m_last = lax.fori_loop(1, n_chunks, pipelined, (m_first, m_first))
    side_work_late()
    value_chunk(n_chunks - 1, (n_chunks - 1) & 1, m_acc, m_last)

    drow = lax.broadcasted_iota(jnp.int32, (LANES, MOBA_BLOCK), 0)
    a0 = acc_ref[0]
    a1 = acc_ref[1]
    out_t = jnp.where(drow < HEAD_DIM, a0 / a0[HEAD_DIM:HEAD_DIM + 1, :], a1 / a1[0:1, :])
    o_ref[...] = out_t.T.astype(o_ref.dtype)


def _prompt_attn(z, cache_kt, page_table, layer):
    rows = z.shape[0]
    q0 = 3 * BRANCH_WIDTH // LANES
    k0 = 4 * BRANCH_WIDTH // LANES
    v0 = 5 * BRANCH_WIDTH // LANES
    n_tiles = BRANCH_WIDTH // LANES
    assert HEADS_PER_TILE == 2 and N_PROMPT_BLOCKS % ATTN_GROUP == 0
    assert BATCH * n_tiles * N_PROMPT_BLOCKS == DEC_BATCH * KM_CHUNKS

    def step(n, hp, b):
        return (n * n_tiles + hp) * N_PROMPT_BLOCKS + b

    def page_spec(r):
        def index_map(n, hp, b, pt):
            s = step(n, hp, b)
            return (layer, pt[s // KM_CHUNKS, (s % KM_CHUNKS) * KM_PAGES + r], 0, 0, 0)
        return pl.BlockSpec((None, None, N_HEADS, HEAD_DIM, PAGE_SIZE), index_map)

    return pl.pallas_call(
        _prompt_attn_kernel,
        grid_spec=pltpu.PrefetchScalarGridSpec(
            num_scalar_prefetch=1,
            grid=(BATCH, n_tiles, N_PROMPT_BLOCKS),
            in_specs=[
                pl.BlockSpec((SEQ, LANES), lambda n, hp, b, pt: (n, q0 + hp)),
                pl.BlockSpec((SEQ, LANES), lambda n, hp, b, pt: (n, k0 + hp)),
                pl.BlockSpec((SEQ, LANES), lambda n, hp, b, pt: (n, v0 + hp)),
            ] + [page_spec(r) for r in range(KM_PAGES)],
            out_specs=[
                pl.BlockSpec((MOBA_BLOCK, LANES), lambda n, hp, b, pt: (n * N_PROMPT_BLOCKS + b, hp)),
                pl.BlockSpec((1, N_HEADS, HEAD_DIM, N_PAST_BLOCKS),
                             lambda n, hp, b, pt: (step(n, hp, b) // KM_CHUNKS, 0, 0, 0)),
            ],
            scratch_shapes=[pltpu.VMEM((N_PROMPT_BLOCKS, MOBA_BLOCK, LANES), BF16),
                            pltpu.VMEM((HEADS_PER_TILE, N_PROMPT_BLOCKS, LANES, MOBA_BLOCK), BF16),
                            pltpu.VMEM((N_PROMPT_BLOCKS, LANES), F32),
                            pltpu.VMEM((HEADS_PER_TILE, N_PROMPT_BLOCKS, N_PROMPT_BLOCKS, MOBA_BLOCK), F32),
                            pltpu.VMEM((HEADS_PER_TILE, 2, ATTN_GROUP, MOBA_BLOCK, MOBA_BLOCK), F32),
                            pltpu.VMEM((HEADS_PER_TILE, LANES, MOBA_BLOCK), F32)],
        ),
        out_shape=[jax.ShapeDtypeStruct((rows, BRANCH_WIDTH), BF16),
                   jax.ShapeDtypeStruct((DEC_BATCH, N_HEADS, HEAD_DIM, N_PAST_BLOCKS), F32)],
        compiler_params=_cparams(3),
        name="prompt_attn",
    )(page_table, z, z, z, *([cache_kt] * KM_PAGES))


def _finish_kernel(x_ref, oab_ref, oc_ref, gates_ref, wb_ref, wo_ref, g_ref, o_ref):
    br_a = jnp.dot(oab_ref[:, 0:BRANCH_WIDTH], wb_ref[0], preferred_element_type=F32)
    br_b = jnp.dot(oab_ref[:, BRANCH_WIDTH:2 * BRANCH_WIDTH], wb_ref[1], preferred_element_type=F32)
    br_c = jnp.dot(oc_ref[...], wb_ref[2], preferred_element_type=F32)
    mixed = (gates_ref[:, 0:D_MODEL] * br_a + gates_ref[:, D_MODEL:2 * D_MODEL] * br_b
             + gates_ref[:, 2 * D_MODEL:3 * D_MODEL] * br_c)
    mix = jnp.dot(mixed.astype(BF16), wo_ref[...], preferred_element_type=F32)
    o_ref[...] = x_ref[...] + _rms(mix, g_ref[...])


def _finish(x2d, oab, oc, z, w_branch_bf, w_out_bf, g_post_mix, tm):
    rows = x2d.shape[0]
    return pl.pallas_call(
        _finish_kernel,
        grid=(rows // tm,),
        in_specs=[
            pl.BlockSpec((tm, D_MODEL), lambda i: (i, 0)),
            pl.BlockSpec((tm, 2 * BRANCH_WIDTH), lambda i: (i, 0)),
            pl.BlockSpec((tm, BRANCH_WIDTH), lambda i: (i, 0)),
            pl.BlockSpec((tm, N_BRANCH * D_MODEL), lambda i: (i, 1)),
            pl.BlockSpec((N_BRANCH, BRANCH_WIDTH, D_MODEL), lambda i: (0, 0, 0)),
            pl.BlockSpec((D_MODEL, D_MODEL), lambda i: (0, 0)),
            pl.BlockSpec((1, D_MODEL), lambda i: (0, 0)),
        ],
        out_specs=pl.BlockSpec((tm, D_MODEL), lambda i: (i, 0)),
        out_shape=jax.ShapeDtypeStruct((rows, D_MODEL), F32),
        compiler_params=_cparams(1),
        name="finish",
    )(x2d, oab, oc, z, w_branch_bf, w_out_bf, g_post_mix)


def _ffn_kernel(x_ref, gpre_ref, wg_ref, wu_ref, wo_ref, gpost_ref, o_ref, xn_ref, acc_ref):
    f = pl.program_id(1)

    @pl.when(f == 0)
    def _():
        xn_ref[...] = _rms(x_ref[...], gpre_ref[...]).astype(BF16)
        acc_ref[...] = jnp.zeros_like(acc_ref)

    xn = xn_ref[...]
    gt = jnp.dot(xn, wg_ref[...], preferred_element_type=F32)
    up = jnp.dot(xn, wu_ref[...], preferred_element_type=F32)
    h = (gt * _sigmoid(gt)) * up
    acc_ref[...] += jnp.dot(h.astype(BF16), wo_ref[...], preferred_element_type=F32)

    @pl.when(f == pl.num_programs(1) - 1)
    def _():
        o_ref[...] = x_ref[...] + _rms(acc_ref[...], gpost_ref[...])


def _ffn(x2d, g_pre, w_ff_in_bf, w_ff_out_bf, g_post, tm):
    rows = x2d.shape[0]
    n_f = D_FF // FF_TILE
    return pl.pallas_call(
        _ffn_kernel,
        grid=(rows // tm, n_f),
        in_specs=[
            pl.BlockSpec((tm, D_MODEL), lambda i, f: (i, 0)),
            pl.BlockSpec((1, D_MODEL), lambda i, f: (0, 0)),
            pl.BlockSpec((D_MODEL, FF_TILE), lambda i, f: (0, f)),
            pl.BlockSpec((D_MODEL, FF_TILE), lambda i, f: (0, n_f + f)),
            pl.BlockSpec((FF_TILE, D_MODEL), lambda i, f: (f, 0)),
            pl.BlockSpec((1, D_MODEL), lambda i, f: (0, 0)),
        ],
        out_specs=pl.BlockSpec((tm, D_MODEL), lambda i, f: (i, 0)),
        out_shape=jax.ShapeDtypeStruct((rows, D_MODEL), F32),
        scratch_shapes=[pltpu.VMEM((tm, D_MODEL), BF16), pltpu.VMEM((tm, D_MODEL), F32)],
        compiler_params=_cparams(2),
        name="ffn",
    )(x2d, g_pre, w_ff_in_bf, w_ff_in_bf, w_ff_out_bf, g_post)


SAMPLE_ROWS = DEC_BATCH * DEC_SEQ


def _sample_mix_kernel(u_ref, vn_ref, p_ref, st_ref, wv_ref, bv_ref, wp_ref, sc_ref, oab_ref):
    def slab(ref, s):
        return ref[s * DEC_BATCH:(s + 1) * DEC_BATCH, :]

    for t in range(DEC_SEQ):
        s_t = bv_ref[t:t + 1, :]
        for sp in range(t + 1):
            s_t = s_t + wv_ref[t, sp:sp + 1, :] * slab(vn_ref, sp)
        oab_ref[t * DEC_BATCH:(t + 1) * DEC_BATCH, 0:BRANCH_WIDTH] = (slab(u_ref, t) * s_t).astype(oab_ref.dtype)

    ext = [st_ref[r] for r in range(POOL_KEEP)] + [slab(p_ref, s) for s in range(DEC_SEQ)]
    lane = lax.broadcasted_iota(jnp.int32, (1, BRANCH_WIDTH), 1)
    d_rows = []
    for s in range(DEC_SEQ):
        end = POOL_KEEP + s + 1
        mean = jnp.zeros((DEC_BATCH, BRANCH_WIDTH), F32)
        run = jnp.zeros((DEC_BATCH, BRANCH_WIDTH), F32)
        taken = 0
        for gi, win in enumerate(POOL_WINDOWS):
            for r in range(end - win, end - taken):
                run = run + ext[r]
            taken = win
            mean = jnp.where((lane // POOL_GROUP_WIDTH) == gi, run / float(win), mean)
        d_rows.append(mean - ext[POOL_KEEP + s])
    d = jnp.concatenate(d_rows, axis=0).astype(BF16)
    for gi in range(len(POOL_WINDOWS)):
        cs = slice(gi * POOL_GROUP_WIDTH, (gi + 1) * POOL_GROUP_WIDTH)
        y = jnp.dot(d[:, cs], wp_ref[gi], preferred_element_type=F32)
        oab_ref[:, BRANCH_WIDTH + gi * POOL_GROUP_WIDTH:BRANCH_WIDTH + (gi + 1) * POOL_GROUP_WIDTH] = (
            y * sc_ref[:, cs]).astype(oab_ref.dtype)


def _sample_mix(z, state_t, w_vec, b_vec, w_pool_bf, pool_scale):
    return pl.pallas_call(
        _sample_mix_kernel,
        grid=(1,),
        in_specs=[
            pl.BlockSpec((SAMPLE_ROWS, BRANCH_WIDTH), lambda i: (0, 0)),
            pl.BlockSpec((SAMPLE_ROWS, BRANCH_WIDTH), lambda i: (0, 1)),
            pl.BlockSpec((SAMPLE_ROWS, BRANCH_WIDTH), lambda i: (0, 2)),
            pl.BlockSpec((POOL_KEEP, DEC_BATCH, BRANCH_WIDTH), lambda i: (0, 0, 0)),
            pl.BlockSpec((DEC_SEQ, DEC_SEQ, BRANCH_WIDTH), lambda i: (0, 0, 0)),
            pl.BlockSpec((DEC_SEQ, BRANCH_WIDTH), lambda i: (0, 0)),
            pl.BlockSpec((len(POOL_WINDOWS), POOL_GROUP_WIDTH, POOL_GROUP_WIDTH), lambda i: (0, 0, 0)),
            pl.BlockSpec((1, BRANCH_WIDTH), lambda i: (0, 0)),
        ],
        out_specs=pl.BlockSpec((SAMPLE_ROWS, 2 * BRANCH_WIDTH), lambda i: (0, 0)),
        out_shape=jax.ShapeDtypeStruct((SAMPLE_ROWS, 2 * BRANCH_WIDTH), BF16),
        compiler_params=_cparams(1),
        name="sample_mix",
    )(z, z, z, state_t, w_vec, b_vec, w_pool_bf, pool_scale)


TOPK_SEQS = 8


def _gate_topk_kernel(q_ref, km_ref, idx_ref):
    lane_f = lax.broadcasted_iota(jnp.int32, (DEC_SEQ, N_PAST_BLOCKS), 1).astype(F32)
    for n in range(TOPK_SEQS):
        for h in range(N_HEADS):
            g = jnp.dot(q_ref[n, h], km_ref[n, h], precision=lax.Precision.HIGHEST,
                        preferred_element_type=F32)
            for r in range(MOBA_TOP_K):
                mx = jnp.max(g, axis=-1, keepdims=True)
                first = jnp.min(jnp.where(g == mx, lane_f, float(N_PAST_BLOCKS)), axis=-1, keepdims=True)
                idx_ref[n, h, :, r:r + 1] = first.astype(jnp.int32)
                g = jnp.where(lane_f == first, NEG, g)


def _gate_topk(q_nh, km_nh):
    return pl.pallas_call(
        _gate_topk_kernel,
        grid=(DEC_BATCH // TOPK_SEQS,),
        in_specs=[
            pl.BlockSpec((TOPK_SEQS, N_HEADS, DEC_SEQ, HEAD_DIM), lambda n: (n, 0, 0, 0)),
            pl.BlockSpec((TOPK_SEQS, N_HEADS, HEAD_DIM, N_PAST_BLOCKS), lambda n: (n, 0, 0, 0)),
        ],
        out_specs=pl.BlockSpec((TOPK_SEQS, N_HEADS, DEC_SEQ, MOBA_TOP_K), lambda n: (n, 0, 0, 0)),
        out_shape=jax.ShapeDtypeStruct((DEC_BATCH, N_HEADS, DEC_SEQ, MOBA_TOP_K), jnp.int32),
        compiler_params=_cparams(1),
        name="gate_topk",
    )(q_nh, km_nh)


N_SEL_PAGES = MOBA_TOP_K * PAGES_PER_BLOCK


N_SLABS = DEC_SEQ * N_SEL_PAGES


def _sample_attn_kernel(pt_ref, idx_ref, q_ref, kn_ref, vn_ref, ck_ref, cv_ref, o_ref,
                        kbuf, vbuf, sem, *, layer):
    n = pl.program_id(0)
    slot = n & 1

    def slab_copies(seq, buf_slot, h, i):
        s, r = divmod(i, N_SEL_PAGES)
        blk = idx_ref[((seq * N_HEADS + h) * DEC_SEQ + s) * MOBA_TOP_K + r // PAGES_PER_BLOCK]
        page = pt_ref[seq, blk * PAGES_PER_BLOCK + r % PAGES_PER_BLOCK]
        return (pltpu.make_async_copy(ck_ref.at[layer, page, h], kbuf.at[buf_slot, h, i], sem.at[buf_slot]),
                pltpu.make_async_copy(cv_ref.at[layer, page, h], vbuf.at[buf_slot, h, i], sem.at[buf_slot]))

    def for_each_slab(seq, buf_slot, fn):
        def per_head(h, carry):
            for i in range(N_SLABS):
                for cp in slab_copies(seq, buf_slot, h, i):
                    fn(cp)
            return carry
        lax.fori_loop(0, N_HEADS, per_head, 0)

    @pl.when(n == 0)
    def _():
        for_each_slab(n, slot, lambda cp: cp.start())

    for_each_slab(n, slot, lambda cp: cp.wait())

    scale = HEAD_DIM ** -0.5
    pos = lax.broadcasted_iota(jnp.int32, (1, DEC_SEQ), 1)
    last = pl.num_programs(0) - 1
    nxt = jnp.minimum(n + 1, last)

    def per_head(h, carry):
        for i in range(N_SLABS):
            for cp in slab_copies(nxt, 1 - slot, h, i):
                cp.start()
        k_new = kn_ref[0, h]
        v_new = vn_ref[0, h]
        for s in range(DEC_SEQ):
            q = q_ref[0, h, :, s:s + 1] * scale
            s_sel = [jnp.sum(q * kbuf[slot, h, s * N_SEL_PAGES + r], axis=0, keepdims=True)
                     for r in range(N_SEL_PAGES)]
            s_own = jnp.where(pos <= s, jnp.sum(q * k_new, axis=0, keepdims=True), NEG)
            m_sel = functools.reduce(jnp.maximum, s_sel)
            m = jnp.maximum(jnp.max(m_sel, axis=-1, keepdims=True), jnp.max(s_own, axis=-1, keepdims=True))
            p_sel = [jnp.exp(sr - m) for sr in s_sel]
            p_own = jnp.exp(s_own - m)
            l = (jnp.sum(functools.reduce(jnp.add, p_sel), axis=-1, keepdims=True)
                 + jnp.sum(p_own, axis=-1, keepdims=True))
            pv = functools.reduce(jnp.add, [vbuf[slot, h, s * N_SEL_PAGES + r] * p_sel[r]
                                            for r in range(N_SEL_PAGES)])
            o = jnp.sum(pv, axis=-1, keepdims=True) + jnp.sum(v_new * p_own, axis=-1, keepdims=True)
            o_ref[0, h, :, s:s + 1] = o / l
        return carry

    lax.fori_loop(0, N_HEADS, per_head, 0)

    @pl.when(n == last)
    def _():
        for_each_slab(nxt, 1 - slot, lambda cp: cp.wait())


def _sample_attn(q_cols, k_cols, v_cols, cache_kt, cache_vt, page_table, idx, layer):
    def new_spec():
        return pl.BlockSpec((1, N_HEADS, HEAD_DIM, DEC_SEQ), lambda n, pt, ix: (n, 0, 0, 0))

    buf = pltpu.VMEM((2, N_HEADS, N_SLABS, HEAD_DIM, PAGE_SIZE), F32)
    return pl.pallas_call(
        functools.partial(_sample_attn_kernel, layer=layer),
        grid_spec=pltpu.PrefetchScalarGridSpec(
            num_scalar_prefetch=2,
            grid=(DEC_BATCH,),
            in_specs=[new_spec(), new_spec(), new_spec(),
                      pl.BlockSpec(memory_space=pl.ANY), pl.BlockSpec(memory_space=pl.ANY)],
            out_specs=new_spec(),
            scratch_shapes=[buf, buf, pltpu.SemaphoreType.DMA((2,))],
        ),
        out_shape=jax.ShapeDtypeStruct((DEC_BATCH, N_HEADS, HEAD_DIM, DEC_SEQ), F32),
        compiler_params=_cparams(1),
        name="sample_attn",
    )(page_table, idx.reshape(-1), q_cols, k_cols, v_cols, cache_kt, cache_vt)


def _rope_tables(pos):
    half = HEAD_DIM // 2
    inv = ROPE_THETA ** (-jnp.arange(half, dtype=F32) / half)
    ang = pos.astype(F32)[:, None] * inv[None, :]
    cos, sin = jnp.cos(ang), jnp.sin(ang)
    reps = LANES // half
    cos_t = jnp.tile(cos, (1, reps))
    sin_t = jnp.concatenate([-sin, sin] * (reps // 2), axis=1)
    return cos_t, sin_t


def _pages_out(per_layer):
    t = jnp.stack(per_layer).reshape(DEPTH, BATCH, SEQ // PAGE_SIZE, N_HEADS, HEAD_DIM, PAGE_SIZE)
    return t.transpose(0, 1, 2, 3, 5, 4)


def _sample_heads(a2d):
    return a2d.reshape(DEC_SEQ, DEC_BATCH, N_HEADS, HEAD_DIM).transpose(1, 2, 0, 3)


def _sample_rows(a2d):
    return a2d.reshape(DEC_SEQ, DEC_BATCH, a2d.shape[-1]).transpose(1, 0, 2)


def kernel(x_prompt, x_sample, cache_k, cache_v, state_pool, page_table, w_in, g_v, w_s, b_s, w_pool, pool_scale, w_branch, w_out, g_pre_mix, g_post_mix, g_pre_ffn, g_post_ffn, w_ff_in, w_ff_out):
    cos_p, sin_p = _rope_tables(jnp.arange(SEQ))
    cos_s, sin_s = _rope_tables(PAST_LEN + jnp.arange(SAMPLE_ROWS) // DEC_BATCH)

    xp = x_prompt.reshape(BATCH * SEQ, D_MODEL)
    xs = x_sample.transpose(1, 0, 2).reshape(SAMPLE_ROWS, D_MODEL)
    page_table = page_table.astype(jnp.int32)
    cache_kt = cache_k.transpose(0, 1, 2, 4, 3)
    cache_vt = cache_v.transpose(0, 1, 2, 4, 3)

    kp_l, vp_l, poolp_l, ks_l, vs_l, pools_l, cvs_l = [], [], [], [], [], [], []
    for l in range(DEPTH):
        w_in_bf = w_in[l].astype(BF16)
        w_pool_bf = w_pool[l].astype(BF16)
        w_branch_bf = w_branch[l].astype(BF16)
        w_out_bf = w_out[l].astype(BF16)
        w_ff_in_bf = w_ff_in[l].astype(BF16)
        w_ff_out_bf = w_ff_out[l].astype(BF16)
        row = lambda a: a[l].reshape(1, -1)
        b_s_t = b_s[l].T

        zp, kt_pages, vt_pages = _inproj(xp, row(g_pre_mix), w_in_bf, row(g_v), cos_p, sin_p,
                                         tm=1024, emit_pages=True, z_dtype=BF16)
        oab = _prompt_mix(zp, w_s[l], b_s_t, w_pool_bf, row(pool_scale))
        oc, km = _prompt_attn(zp, cache_kt, page_table, l)
        x1 = _finish(xp, oab, oc, zp, w_branch_bf, w_out_bf, row(g_post_mix), tm=512)
        xp = _ffn(x1, row(g_pre_ffn), w_ff_in_bf, w_ff_out_bf, row(g_post_ffn), tm=512)
        kp_l.append(kt_pages)
        vp_l.append(vt_pages)
        poolp_l.append(zp[:, 2 * BRANCH_WIDTH:3 * BRANCH_WIDTH]
                       .reshape(BATCH, SEQ, BRANCH_WIDTH)[:, SEQ - POOL_KEEP:].astype(F32))

        zs = _inproj(xs, row(g_pre_mix), w_in_bf, row(g_v), cos_s, sin_s, tm=SAMPLE_ROWS,
                     emit_pages=False, z_dtype=F32)
        w_vec = jnp.repeat(w_s[l][:, :DEC_SEQ, :DEC_SEQ].transpose(1, 2, 0), CM_GROUP_WIDTH, axis=-1)
        b_vec = jnp.repeat(b_s[l][:, :DEC_SEQ].T, CM_GROUP_WIDTH, axis=-1)
        state_t = state_pool[l].transpose(1, 0, 2)
        oab_s = _sample_mix(zs, state_t, w_vec, b_vec, w_pool_bf, row(pool_scale))
        q_nh = _sample_heads(zs[:, 3 * BRANCH_WIDTH:4 * BRANCH_WIDTH])
        k_nh = _sample_heads(zs[:, 4 * BRANCH_WIDTH:5 * BRANCH_WIDTH])
        v_nh = _sample_heads(zs[:, 5 * BRANCH_WIDTH:6 * BRANCH_WIDTH])
        idx = _gate_topk(q_nh, km)
        to_cols = lambda a: a.transpose(0, 1, 3, 2)
        oc_cols = _sample_attn(to_cols(q_nh), to_cols(k_nh), to_cols(v_nh), cache_kt, cache_vt,
                               page_table, idx, l)
        oc_s = oc_cols.transpose(3, 0, 1, 2).reshape(SAMPLE_ROWS, BRANCH_WIDTH).astype(BF16)
        x1s = _finish(xs, oab_s, oc_s, zs, w_branch_bf, w_out_bf, row(g_post_mix), tm=SAMPLE_ROWS)
        xs = _ffn(x1s, row(g_pre_ffn), w_ff_in_bf, w_ff_out_bf, row(g_post_ffn), tm=SAMPLE_ROWS)
        ks_l.append(k_nh)
        vs_l.append(v_nh)
        p_s = _sample_rows(zs[:, 2 * BRANCH_WIDTH:3 * BRANCH_WIDTH])
        pools_l.append(jnp.concatenate([state_pool[l][:, DEC_SEQ:], p_s], axis=1))
        cvs_l.append(_sample_rows(zs[:, BRANCH_WIDTH:2 * BRANCH_WIDTH]))

    y_prompt = xp.reshape(BATCH, SEQ, D_MODEL)
    y_sample = _sample_rows(xs)
    return (y_prompt, y_sample, _pages_out(kp_l), _pages_out(vp_l), jnp.stack(poolp_l),
            jnp.stack(ks_l), jnp.stack(vs_l), jnp.stack(pools_l), jnp.stack(cvs_l))
```
